```python
import jax, jax.numpy as jnp
from jax import lax
import numpy as np

D_MODEL = 2048
BATCH = 4
SEQ = 4096
DEPTH = 1

D_MIX = D_MODEL
D_LRU = D_MIX // 2
LRU_BLOCKS = 8
LRU_BLOCK = D_LRU // LRU_BLOCKS
CONV_WIDTH = 4
LRU_C = 8.0
D_HGRN = D_MIX - D_LRU
HGRN_EXPAND = 128
HGRN_HEADS = D_HGRN // HGRN_EXPAND
HGRN_DK = HGRN_EXPAND
HGRN_DV = D_HGRN // HGRN_HEADS
CHUNK = 64
IN_WIDTHS = (D_LRU, D_LRU, HGRN_HEADS * HGRN_DK, HGRN_HEADS * HGRN_DK, D_HGRN, D_HGRN)
IN_COLS = sum(IN_WIDTHS)
N_EXPERTS = 256
TOP_K = 8
N_GROUPS = 8
TOPK_GROUPS = 4
D_EXPERT = D_MODEL // 4
ROUTED_SCALE = 2.5
EXPERT_BLOCK = 128
RMS_EPS = 1e-6

kernel_name = "hybrid_rglru_hgrn2_moe_adaln"


def rms_norm(x, g):
    x32 = x.astype(jnp.float32)
    y = x32 * lax.rsqrt(jnp.mean(x32 * x32, axis=-1, keepdims=True) + RMS_EPS)
    return (y * g.astype(jnp.float32)).astype(x.dtype)


def rg_lru_group(u, z, conv_w, conv_b, wa, ba, wx, bx, lam):
    B, L, C = u.shape
    f32 = jnp.float32
    xc = lax.conv_general_dilated(u, conv_w[:, None, :], window_strides=(1,),
                                  padding=[(CONV_WIDTH - 1, 0)],
                                  dimension_numbers=('NWC', 'WIO', 'NWC'),
                                  feature_group_count=C) + conv_b
    xb = xc.reshape(B, L, LRU_BLOCKS, LRU_BLOCK)
    r = jax.nn.sigmoid((jnp.einsum('blhi,hij->blhj', xb, wa).reshape(B, L, C) + ba).astype(f32))
    i = jax.nn.sigmoid((jnp.einsum('blhi,hij->blhj', xb, wx).reshape(B, L, C) + bx).astype(f32))
    log_a = -LRU_C * r * jax.nn.softplus(-lam.astype(f32))
    a = jnp.exp(log_a)
    mult = jnp.sqrt(1.0 - jnp.exp(2.0 * log_a))
    mult = jnp.where((jnp.arange(L) == 0)[None, :, None], 1.0, mult)
    b = mult * i * xc.astype(f32)

    def combine(lhs, rhs):
        a1, b1 = lhs
        a2, b2 = rhs
        return a1 * a2, a2 * b1 + b2

    _, h = lax.associative_scan(combine, (a, b), axis=1)
    return (h * jax.nn.gelu(z.astype(f32))).astype(u.dtype)


def chunked_gated_recurrence(q, k, v, log_f):
    B, L, H, dk = q.shape
    dv = v.shape[-1]
    N = L // CHUNK

    def to_chunks(t):
        return t.reshape(B, N, CHUNK, H, t.shape[-1]).transpose(1, 0, 3, 2, 4)

    qc, kc, vc, gc = to_chunks(q), to_chunks(k), to_chunks(v), to_chunks(log_f)
    causal = jnp.tril(jnp.ones((CHUNK, CHUNK), bool))[:, :, None]

    def step(S, inp):
        q_, k_, v_, g_ = inp
        bcum = jnp.cumsum(g_, axis=2)
        o_inter = jnp.einsum('bhtk,bhkv->bhtv', q_ * jnp.exp(bcum), S)
        diff = bcum[:, :, :, None, :] - bcum[:, :, None, :, :]
        decay = jnp.exp(jnp.where(causal, diff, -jnp.inf))
        scores = jnp.sum(q_[:, :, :, None, :] * decay * k_[:, :, None, :, :], axis=-1)
        o_intra = jnp.einsum('bhts,bhsv->bhtv', scores, v_)
        b_last = bcum[:, :, -1:, :]
        S_new = jnp.exp(b_last[:, :, 0, :])[..., None] * S + \
            jnp.einsum('bhsk,bhsv->bhkv', k_ * jnp.exp(b_last - bcum), v_)
        return S_new, o_inter + o_intra

    S0 = jnp.zeros((B, H, dk, dv), q.dtype)
    _, o = lax.scan(step, S0, (qc, kc, vc, gc))
    return o.transpose(1, 0, 3, 2, 4).reshape(B, L, H, dv)


def hgrn2_group(q_in, f_in, v_in, g_in, lb, norm_g):
    B, L, _ = q_in.shape
    f32 = jnp.float32
    q = jax.nn.silu(q_in.astype(f32)).reshape(B, L, HGRN_HEADS, HGRN_DK)
    f = lb + (1.0 - lb) * jax.nn.sigmoid(f_in.astype(f32))
    log_f = jnp.log(f).reshape(B, L, HGRN_HEADS, HGRN_DK)
    k = (1.0 - f).reshape(B, L, HGRN_HEADS, HGRN_DK)
    v = v_in.astype(f32).reshape(B, L, HGRN_HEADS, HGRN_DV)
    o = chunked_gated_recurrence(q, k, v, log_f)
    o = o * lax.rsqrt(jnp.mean(o * o, axis=-1, keepdims=True) + RMS_EPS) * norm_g.astype(f32)
    o = o.reshape(B, L, D_HGRN) * jax.nn.silu(g_in.astype(f32))
    return o.astype(q_in.dtype)


def moe_ffn(h, w_router, router_bias, w_gate, w_up, w_down, ws_gate, ws_up, ws_down):
    B, L, D = h.shape
    T = B * L
    E, M, K = N_EXPERTS, EXPERT_BLOCK, TOP_K
    f32 = jnp.float32
    hf = h.reshape(T, D)
    scores = jax.nn.sigmoid(jnp.matmul(hf, w_router).astype(f32))
    sel = scores + router_bias.astype(f32)
    grp_score = jnp.sum(lax.top_k(sel.reshape(T, N_GROUPS, E // N_GROUPS), 2)[0], axis=-1)
    _, grp_idx = lax.top_k(grp_score, TOPK_GROUPS)
    grp_mask = jnp.any(grp_idx[:, :, None] == jnp.arange(N_GROUPS)[None, None, :], axis=1)
    sel = jnp.where(jnp.repeat(grp_mask, E // N_GROUPS, axis=1), sel, -jnp.inf)
    _, idx = lax.top_k(sel, K)
    w = jnp.take_along_axis(scores, idx, axis=1)
    w = w / jnp.sum(w, axis=-1, keepdims=True) * ROUTED_SCALE

    NK = T * K
    R = -(-(NK + E * (M - 1)) // M) * M
    NB = R // M
    flat_e = idx.reshape(-1)
    flat_tok = jnp.repeat(jnp.arange(T, dtype=jnp.int32), K)
    flat_w = w.reshape(-1)
    order = jnp.argsort(flat_e, stable=True)
    se = flat_e[order]
    counts = jnp.bincount(flat_e, length=E)
    starts = jnp.cumsum(counts) - counts
    pcounts = (counts + M - 1) // M * M
    pends = jnp.cumsum(pcounts)
    pstarts = pends - pcounts
    dest = pstarts[se] + jnp.arange(NK) - starts[se]
    row_tok = jnp.zeros((R,), jnp.int32).at[dest].set(flat_tok[order])
    row_w = jnp.zeros((R,), f32).at[dest].set(flat_w[order])
    block_e = jnp.minimum(jnp.searchsorted(pends, jnp.arange(NB) * M, side='right'), E - 1)

    def block_step(acc, blk):
        tok, wt, e = blk
        xb = hf[tok]
        a = jax.nn.silu(xb @ w_gate[e]) * (xb @ w_up[e])
        yb = (a @ w_down[e]).astype(f32) * wt[:, None]
        return acc.at[tok].add(yb), None

    routed, _ = lax.scan(block_step, jnp.zeros((T, D), f32),
                         (row_tok.reshape(NB, M), row_w.reshape(NB, M), block_e))
    shared = (jax.nn.silu(hf @ ws_gate) * (hf @ ws_up)) @ ws_down
    return (routed + shared.astype(f32)).astype(h.dtype).reshape(B, L, D)


def setup_inputs(seed: int = 0) -> dict:
    key = jax.random.key(seed)
    ks = jax.random.split(key, 26)
    f32 = jnp.float32
    D, E, F = D_MODEL, N_EXPERTS, D_EXPERT

    def nrm(k, shape, std):
        return jax.random.normal(k, shape, f32) * std

    u = jax.random.uniform(ks[12], (DEPTH, D_LRU), f32, 0.9, 0.999)
    a = u ** (1.0 / LRU_C)
    lru_lambda = jnp.log(a) - jnp.log1p(-a)
    return {
        "x": nrm(ks[0], (BATCH, SEQ, D), 1.0),
        "c": nrm(ks[1], (BATCH, D), 1.0),
        "w_ada": nrm(ks[2], (DEPTH, D, 6 * D), 0.5 * D ** -0.5),
        "b_ada": nrm(ks[3], (DEPTH, 6 * D), 0.02),
        "norm1_g": 1.0 + nrm(ks[4], (DEPTH, D), 0.1),
        "w_in": nrm(ks[5], (DEPTH, D, IN_COLS), D ** -0.5),
        "conv_w": nrm(ks[6], (DEPTH, CONV_WIDTH, D_LRU), CONV_WIDTH ** -0.5),
        "conv_b": nrm(ks[7], (DEPTH, D_LRU), 0.02),
        "lru_wa": nrm(ks[8], (DEPTH, LRU_BLOCKS, LRU_BLOCK, LRU_BLOCK), LRU_BLOCK ** -0.5),
        "lru_ba": nrm(ks[9], (DEPTH, D_LRU), 0.02),
        "lru_wx": nrm(ks[10], (DEPTH, LRU_BLOCKS, LRU_BLOCK, LRU_BLOCK), LRU_BLOCK ** -0.5),
        "lru_bx": nrm(ks[11], (DEPTH, D_LRU), 0.02),
        "lru_lambda": lru_lambda,
        "hgrn_lb": nrm(ks[13], (DEPTH + 1, HGRN_HEADS * HGRN_DK), 0.1),
        "hgrn_norm_g": 1.0 + nrm(ks[14], (DEPTH, HGRN_DV), 0.1),
        "w_out": nrm(ks[15], (DEPTH, D_MIX, D), D_MIX ** -0.5),
        "norm2_g": 1.0 + nrm(ks[16], (DEPTH, D), 0.1),
        "w_router": nrm(ks[17], (DEPTH, D, E), D ** -0.5),
        "router_bias": nrm(ks[18], (DEPTH, E), 0.01),
        "w_gate": nrm(ks[19], (DEPTH, E, D, F), D ** -0.5),
        "w_up": nrm(ks[20], (DEPTH, E, D, F), D ** -0.5),
        "w_down": nrm(ks[21], (DEPTH, E, F, D), F ** -0.5),
        "ws_gate": nrm(ks[22], (DEPTH, D, F), D ** -0.5),
        "ws_up": nrm(ks[23], (DEPTH, D, F), D ** -0.5),
        "ws_down": nrm(ks[24], (DEPTH, F, D), F ** -0.5),
        "final_g": 1.0 + nrm(ks[25], (D,), 0.1),
    }


def reference(x, c, w_ada, b_ada, norm1_g, w_in, conv_w, conv_b, lru_wa, lru_ba, lru_wx, lru_bx,
              lru_lambda, hgrn_lb, hgrn_norm_g, w_out, norm2_g, w_router, router_bias,
              w_gate, w_up, w_down, ws_gate, ws_up, ws_down, final_g):
    lower_bounds = jnp.cumsum(jax.nn.softmax(hgrn_lb.astype(jnp.float32), axis=0), axis=0)
    split_at = [int(s) for s in np.cumsum(IN_WIDTHS)[:-1]]
    cond = jax.nn.silu(c)
    for l in range(DEPTH):
        mod = jnp.matmul(cond, w_ada[l]) + b_ada[l]
        sh1, sc1, gt1, sh2, sc2, gt2 = jnp.split(mod[:, None, :], 6, axis=-1)
        h = rms_norm(x, norm1_g[l]) * (1.0 + sc1) + sh1
        proj = jnp.matmul(h, w_in[l])
        u, z, q_in, f_in, v_in, g_in = jnp.split(proj, split_at, axis=-1)
        y_lru = rg_lru_group(u, z, conv_w[l], conv_b[l], lru_wa[l], lru_ba[l],
                             lru_wx[l], lru_bx[l], lru_lambda[l])
        y_hgrn = hgrn2_group(q_in, f_in, v_in, g_in, lower_bounds[l], hgrn_norm_g[l])
        mix = jnp.matmul(jnp.concatenate([y_lru, y_hgrn], axis=-1), w_out[l])
        x = x + gt1 * mix
        h2 = rms_norm(x, norm2_g[l]) * (1.0 + sc2) + sh2
        x = x + gt2 * moe_ffn(h2, w_router[l], router_bias[l], w_gate[l], w_up[l], w_down[l],
                              ws_gate[l], ws_up[l], ws_down[l])
    return rms_norm(x, final_g)
```

```python
import functools

import jax
import jax.numpy as jnp
from jax import lax
from jax.experimental import pallas as pl
from jax.experimental.pallas import tpu as pltpu

F32 = jnp.float32
BF16 = jnp.bfloat16
I32 = jnp.int32

RMS_EPS = 1e-6
LRU_C = 8.0
LRU_BLOCK = 128
CONV_WIDTH = 4
HEAD_DIM = 128
CHUNK = 64
SUB = 16
N_GROUPS = 8
TOPK_GROUPS = 4
TOP_K = 8
ROUTED_SCALE = 2.5
EXPERT_ROWS = 256
MIB = 1024 * 1024


def _params(sem, vmem_mib):
    return pltpu.CompilerParams(dimension_semantics=sem, vmem_limit_bytes=vmem_mib * MIB)


def _sigmoid(x):
    return 1.0 / (1.0 + jnp.exp(-x))


def _dot(a, b):
    return jnp.dot(a, b, preferred_element_type=F32)


def _dot_nt(a, b):
    return lax.dot_general(a, b, (((1,), (1,)), ((), ())), preferred_element_type=F32)


def _dot_tn(a, b):
    return lax.dot_general(a, b, (((0,), (0,)), ((), ())), preferred_element_type=F32)


def _mod_kernel(c_ref, w_ref, b_ref, o_ref):
    c = c_ref[...]
    cond = c * _sigmoid(c)
    o_ref[...] = _dot(cond.astype(BF16), w_ref[...].astype(BF16)) + b_ref[...]


def _adaln_mod(c_pad, w_ada, b_ada):
    rows, d = c_pad.shape
    n = w_ada.shape[1]
    bn = 1024
    return pl.pallas_call(
        _mod_kernel,
        grid=(n // bn,),
        in_specs=[pl.BlockSpec((rows, d), lambda j: (0, 0)),
                  pl.BlockSpec((d, bn), lambda j: (0, j)),
                  pl.BlockSpec((1, bn), lambda j: (0, j))],
        out_specs=pl.BlockSpec((rows, bn), lambda j: (0, j)),
        out_shape=jax.ShapeDtypeStruct((rows, n), F32),
        compiler_params=_params(("arbitrary",), 40),
    )(c_pad, w_ada, b_ada)


def _norm_mod_kernel(x_ref, g_ref, sc_ref, sh_ref, o_ref):
    x = x_ref[0]
    ms = jnp.mean(x * x, axis=-1, keepdims=True)
    y = x * lax.rsqrt(ms + RMS_EPS) * g_ref[...]
    o_ref[...] = (y * (1.0 + sc_ref[0]) + sh_ref[0]).astype(o_ref.dtype)


def _norm_mod(x, g, sc, sh, tn):
    b, l, d = x.shape
    nt = l // tn
    return pl.pallas_call(
        _norm_mod_kernel,
        grid=(b, nt),
        in_specs=[pl.BlockSpec((1, tn, d), lambda i, j: (i, j, 0)),
                  pl.BlockSpec((1, d), lambda i, j: (0, 0)),
                  pl.BlockSpec((1, 1, d), lambda i, j: (i, 0, 0)),
                  pl.BlockSpec((1, 1, d), lambda i, j: (i, 0, 0))],
        out_specs=pl.BlockSpec((tn, d), lambda i, j: (i * nt + j, 0)),
        out_shape=jax.ShapeDtypeStruct((b * l, d), BF16),
        compiler_params=_params(("arbitrary", "arbitrary"), 40),
    )(x, g, sc, sh)


def _mm_kernel(x_ref, w_ref, o_ref):
    o_ref[...] = _dot(x_ref[...], w_ref[...]).astype(o_ref.dtype)


def _matmul(x, w, bm, bn, out_dtype):
    m, k = x.shape
    n = w.shape[1]
    return pl.pallas_call(
        _mm_kernel,
        grid=(n // bn, m // bm),
        in_specs=[pl.BlockSpec((bm, k), lambda j, i: (i, 0)),
                  pl.BlockSpec((k, bn), lambda j, i: (0, j))],
        out_specs=pl.BlockSpec((bm, bn), lambda j, i: (i, j)),
        out_shape=jax.ShapeDtypeStruct((m, n), out_dtype),
        compiler_params=_params(("arbitrary", "arbitrary"), 48),
    )(x, w)


def _shift_rows(u, prev, k, row8):
    if k == 0:
        return u
    rolled = pltpu.roll(u, k, 0)
    head = jnp.where(row8 < k, pltpu.roll(prev, k, 0), rolled[:8])
    return jnp.concatenate([head, rolled[8:]], axis=0)


def _lru_kernel(u_ref, z_ref, cw_ref, cb_ref, wa_ref, ba_ref, wx_ref, bx_ref, lam_ref,
                o_ref, prev_ref, h_ref):
    j = pl.program_id(1)
    tt, c = u_ref.shape

    @pl.when(j == 0)
    def _():
        prev_ref[...] = jnp.zeros_like(prev_ref)
        h_ref[...] = jnp.zeros_like(h_ref)

    u = u_ref[...].astype(F32)
    prev = prev_ref[...]
    row8 = lax.broadcasted_iota(I32, (8, c), 0)
    xc = jnp.broadcast_to(cb_ref[...], (tt, c))
    for w in range(CONV_WIDTH):
        xc = xc + cw_ref[w:w + 1, :] * _shift_rows(u, prev, CONV_WIDTH - 1 - w, row8)
    prev_ref[...] = u[tt - 8:, :]

    xcb = xc.astype(BF16)
    ra, rx = [], []
    for blk in range(c // LRU_BLOCK):
        xs = xcb[:, blk * LRU_BLOCK:(blk + 1) * LRU_BLOCK]
        ra.append(_dot(xs, wa_ref[blk]))
        rx.append(_dot(xs, wx_ref[blk]))
    r = _sigmoid(jnp.concatenate(ra, axis=1) + ba_ref[...])
    ig = _sigmoid(jnp.concatenate(rx, axis=1) + bx_ref[...])

    nl = -lam_ref[...]
    softplus = jnp.maximum(nl, 0.0) + jnp.log1p(jnp.exp(-jnp.abs(nl)))
    log_a = (-LRU_C) * r * softplus
    a = jnp.exp(log_a)
    mult = jnp.sqrt(1.0 - jnp.exp(2.0 * log_a))
    row = lax.broadcasted_iota(I32, (tt, c), 0)
    mult = jnp.where((row == 0) & (j == 0), 1.0, mult)
    bv = mult * ig * xc

    s = 1
    while s < tt:
        keep = row >= s
        bv = jnp.where(keep, a * pltpu.roll(bv, s, 0) + bv, bv)
        a = jnp.where(keep, a * pltpu.roll(a, s, 0), a)
        s *= 2
    h = bv + a * h_ref[...]
    h_ref[...] = h[tt - 1:, :]

    z = z_ref[...].astype(F32)
    gelu = 0.5 * z * (1.0 + jnp.tanh(0.7978845608028654 * (z + 0.044715 * (z * z * z))))
    o_ref[...] = (h * gelu).astype(o_ref.dtype)


def _rg_lru(proj, conv_w, conv_b, wa, ba, wx, bx, lam, batch, seq, tt):
    t = proj.shape[0]
    c = conv_w.shape[1]
    nt = seq // tt
    vec = lambda: pl.BlockSpec((1, c), lambda i, j: (0, 0))
    mat = lambda: pl.BlockSpec(wa.shape, lambda i, j: (0, 0, 0))
    return pl.pallas_call(
        _lru_kernel,
        grid=(batch, nt),
        in_specs=[pl.BlockSpec((tt, c), lambda i, j: (i * nt + j, 0)),
                  pl.BlockSpec((tt, c), lambda i, j: (i * nt + j, 1)),
                  pl.BlockSpec((CONV_WIDTH, c), lambda i, j: (0, 0)),
                  vec(), mat(), vec(), mat(), vec(), vec()],
        out_specs=pl.BlockSpec((tt, c), lambda i, j: (i * nt + j, 0)),
        out_shape=jax.ShapeDtypeStruct((t, c), BF16),
        scratch_shapes=[pltpu.VMEM((8, c), F32), pltpu.VMEM((1, c), F32)],
        compiler_params=_params(("arbitrary", "arbitrary"), 48),
    )(proj, proj, conv_w, conv_b, wa, ba, wx, bx, lam)


def _hgrn_chunk(qi, fi, v, gi, lb, ng, st, ones_b):
    n = CHUNK
    q = qi * _sigmoid(qi)
    f = lb + (1.0 - lb) * _sigmoid(fi)
    g = jnp.log(f)
    k = 1.0 - f
    row = lax.broadcasted_iota(I32, (n, HEAD_DIM), 0)
    b = g
    s = 1
    while s < n:
        b = b + jnp.where(row >= s, pltpu.roll(b, s, 0), 0.0)
        s *= 2
    o_state = _dot_nt((q * jnp.exp(b)).astype(BF16), st.astype(BF16))
    nsub = n // SUB
    refb = jnp.concatenate(
        [jnp.broadcast_to(b[i * SUB:i * SUB + 1, :], (SUB, HEAD_DIM)) for i in range(nsub)], axis=0)
    qt = (q * jnp.exp(b - refb)).astype(BF16)
    vb = v.astype(BF16)
    row_s = lax.broadcasted_iota(I32, (SUB, HEAD_DIM), 0)
    outs = []
    for i in range(nsub):
        lo = i * SUB
        acc = o_state[lo:lo + SUB, :]
        if i > 0:
            kt = (k[:lo, :] * jnp.exp(b[lo:lo + 1, :] - b[:lo, :])).astype(BF16)
            sc = _dot_nt(qt[lo:lo + SUB, :], kt)
            acc = acc + _dot(sc.astype(BF16), vb[:lo, :])
        qq = q[lo:lo + SUB, :]
        bq = b[lo:lo + SUB, :]
        prods = []
        for s_ in range(SUB):
            e = jnp.exp(jnp.minimum(bq - b[lo + s_:lo + s_ + 1, :], 0.0))
            p = qq * e * k[lo + s_:lo + s_ + 1, :]
            prods.append(jnp.where(row_s >= s_, p, 0.0).astype(BF16))
        red = _dot(jnp.concatenate(prods, axis=0), ones_b)
        for s_ in range(SUB):
            acc = acc + red[s_ * SUB:(s_ + 1) * SUB, :] * v[lo + s_:lo + s_ + 1, :]
        outs.append(acc)
    o = jnp.concatenate(outs, axis=0)
    b_last = b[n - 1:n, :]
    kd = (k * jnp.exp(b_last - b)).astype(BF16)
    st_new = st * jnp.exp(b_last) + _dot_tn(vb, kd)
    ms = jnp.mean(o * o, axis=-1, keepdims=True)
    y = o * lax.rsqrt(ms + RMS_EPS) * ng * (gi * _sigmoid(gi))
    return y, st_new


def _hgrn_kernel(q_ref, f_ref, v_ref, g_ref, lb_ref, ng_ref, o_ref, st_ref):
    @pl.when(pl.program_id(2) == 0)
    def _():
        st_ref[...] = jnp.zeros_like(st_ref)

    lb = lb_ref[...]
    ng = ng_ref[...]
    ones_b = jnp.ones((HEAD_DIM, HEAD_DIM), BF16)

    def body(ci, carry):
        r0 = pl.multiple_of(ci * CHUNK, CHUNK)
        sl = pl.ds(r0, CHUNK)
        y, st_new = _hgrn_chunk(q_ref[sl, :].astype(F32), f_ref[sl, :].astype(F32),
                                v_ref[sl, :].astype(F32), g_ref[sl, :].astype(F32),
                                lb, ng, st_ref[...], ones_b)
        st_ref[...] = st_new
        o_ref[sl, :] = y.astype(o_ref.dtype)
        return carry

    lax.fori_loop(0, q_ref.shape[0] // CHUNK, body, 0)


def _hgrn2(proj, lb, norm_g, batch, seq, heads, col0, tt):
    t = proj.shape[0]
    nt = seq // tt
    col = lambda off: pl.BlockSpec((tt, HEAD_DIM), lambda i, h, j: (i * nt + j, col0 + off * heads + h))
    return pl.pallas_call(
        _hgrn_kernel,
        grid=(batch, heads, nt),
        in_specs=[col(0), col(1), col(2), col(3),
                  pl.BlockSpec((1, HEAD_DIM), lambda i, h, j: (0, h)),
                  pl.BlockSpec((1, HEAD_DIM), lambda i, h, j: (0, 0))],
        out_specs=pl.BlockSpec((tt, HEAD_DIM), lambda i, h, j: (i * nt + j, h)),
        out_shape=jax.ShapeDtypeStruct((t, heads * HEAD_DIM), BF16),
        scratch_shapes=[pltpu.VMEM((HEAD_DIM, HEAD_DIM), F32)],
        compiler_params=_params(("arbitrary", "arbitrary", "arbitrary"), 32),
    )(proj, proj, proj, proj, lb, norm_g)


def _out_kernel(yl_ref, yh_ref, wl_ref, wh_ref, x_ref, gt_ref, g2_ref, sc_ref, sh_ref, x1_ref, h2_ref):
    mix = _dot(yl_ref[...], wl_ref[...]) + _dot(yh_ref[...], wh_ref[...])
    x1 = x_ref[0] + gt_ref[0] * mix
    x1_ref[...] = x1
    ms = jnp.mean(x1 * x1, axis=-1, keepdims=True)
    y = x1 * lax.rsqrt(ms + RMS_EPS) * g2_ref[...]
    h2_ref[...] = y * (1.0 + sc_ref[0]) + sh_ref[0]


def _out_proj(yl, yh, wl, wh, x, gt1, g2, sc2, sh2, bm):
    b, l, d = x.shape
    t = b * l
    nt = l // bm
    cl, ch = yl.shape[1], yh.shape[1]
    bvec = lambda: pl.BlockSpec((1, 1, d), lambda i, j: (i, 0, 0))
    return pl.pallas_call(
        _out_kernel,
        grid=(b, nt),
        in_specs=[pl.BlockSpec((bm, cl), lambda i, j: (i * nt + j, 0)),
                  pl.BlockSpec((bm, ch), lambda i, j: (i * nt + j, 0)),
                  pl.BlockSpec((cl, d), lambda i, j: (0, 0)),
                  pl.BlockSpec((ch, d), lambda i, j: (0, 0)),
                  pl.BlockSpec((1, bm, d), lambda i, j: (i, j, 0)),
                  bvec(),
                  pl.BlockSpec((1, d), lambda i, j: (0, 0)),
                  bvec(), bvec()],
        out_specs=[pl.BlockSpec((bm, d), lambda i, j: (i * nt + j, 0)),
                   pl.BlockSpec((bm, d), lambda i, j: (i * nt + j, 0))],
        out_shape=[jax.ShapeDtypeStruct((t, d), F32), jax.ShapeDtypeStruct((t, d), F32)],
        compiler_params=_params(("arbitrary", "arbitrary"), 56),
    )(yl, yh, wl, wh, x, gt1, g2, sc2, sh2)


def _route_kernel(h_ref, wr_ref, bias_ref, idx_ref, w_ref, rank_ref, cnt_ref, carry_ref):
    step = pl.program_id(0)
    tm = h_ref.shape[0]
    ne = wr_ref.shape[0]
    gsz = ne // N_GROUPS
    neg = -jnp.inf

    @pl.when(step == 0)
    def _():
        carry_ref[...] = jnp.zeros_like(carry_ref)

    logits = _dot_nt(wr_ref[...], h_ref[...].astype(BF16))
    scores = _sigmoid(logits)
    sel = scores + bias_ref[...]

    sel3 = sel.reshape(N_GROUPS, gsz, tm)
    pos3 = lax.broadcasted_iota(I32, (N_GROUPS, gsz, tm), 1)
    m1 = jnp.max(sel3, axis=1, keepdims=True)
    i1 = jnp.min(jnp.where(sel3 == m1, pos3, gsz), axis=1, keepdims=True)
    m2 = jnp.max(jnp.where(pos3 == i1, neg, sel3), axis=1, keepdims=True)
    gs = (m1 + m2).reshape(N_GROUPS, tm)

    gidx = lax.broadcasted_iota(I32, (N_GROUPS, tm), 0)
    beaten = jnp.zeros((N_GROUPS, tm), I32)
    for gp in range(N_GROUPS):
        other = gs[gp:gp + 1, :]
        beats = (other > gs) | ((other == gs) & (gp < gidx))
        beaten = beaten + beats.astype(I32)
    gkeep = (beaten < TOPK_GROUPS).reshape(N_GROUPS, 1, tm)
    cur = jnp.where(gkeep, sel3, neg).reshape(ne, tm)

    eidx = lax.broadcasted_iota(I32, (ne, tm), 0)
    picked = jnp.zeros((ne, tm), jnp.bool_)
    idx_rows, w_rows = [], []
    for _ in range(TOP_K):
        m = jnp.max(cur, axis=0, keepdims=True)
        ik = jnp.min(jnp.where(cur == m, eidx, ne), axis=0, keepdims=True)
        hit = eidx == ik
        w_rows.append(jnp.sum(jnp.where(hit, scores, 0.0), axis=0, keepdims=True))
        idx_rows.append(ik)
        cur = jnp.where(hit, neg, cur)
        picked = picked | hit
    w = jnp.concatenate(w_rows, axis=0)
    idx_ref[...] = jnp.concatenate(idx_rows, axis=0)
    w_ref[...] = w / jnp.sum(w, axis=0, keepdims=True) * ROUTED_SCALE

    pf = picked.astype(F32)
    ta = lax.broadcasted_iota(I32, (tm, tm), 0)
    tb = lax.broadcasted_iota(I32, (tm, tm), 1)
    before = (ta < tb).astype(BF16)
    cnt = _dot(pf.astype(BF16), before) + carry_ref[...]
    rank_rows = [jnp.sum(jnp.where(eidx == ik, cnt, 0.0), axis=0, keepdims=True) for ik in idx_rows]
    rank_ref[...] = jnp.concatenate(rank_rows, axis=0).astype(I32)
    total = carry_ref[...] + jnp.sum(pf, axis=1, keepdims=True)
    carry_ref[...] = total
    cnt_ref[...] = jnp.broadcast_to(total, cnt_ref.shape).astype(I32)


def _route(h2, wr_t, bias_col, tm):
    t, d = h2.shape
    ne = wr_t.shape[0]
    row = lambda: pl.BlockSpec((TOP_K, tm), lambda i: (0, i))
    return pl.pallas_call(
        _route_kernel,
        grid=(t // tm,),
        in_specs=[pl.BlockSpec((tm, d), lambda i: (i, 0)),
                  pl.BlockSpec((ne, d), lambda i: (0, 0)),
                  pl.BlockSpec((ne, 1), lambda i: (0, 0))],
        out_specs=[row(), row(), row(), pl.BlockSpec((ne, 128), lambda i: (0, 0))],
        out_shape=[jax.ShapeDtypeStruct((TOP_K, t), I32), jax.ShapeDtypeStruct((TOP_K, t), F32),
                   jax.ShapeDtypeStruct((TOP_K, t), I32), jax.ShapeDtypeStruct((ne, 128), I32)],
        scratch_shapes=[pltpu.VMEM((ne, 1), F32)],
        compiler_params=_params(("arbitrary",), 40),
    )(h2, wr_t, bias_col)


def _dispatch_kernel(dest_ref, h_ref, xs_in_ref, xs_ref, sem):
    del xs_in_ref
    tm = h_ref.shape[0]

    def copy(t, k):
        return pltpu.make_async_copy(h_ref.at[pl.ds(t, 1), :],
                                     xs_ref.at[pl.ds(dest_ref[k, t], 1), :], sem)

    def start(t, carry):
        for k in range(TOP_K):
            copy(t, k).start()
        return carry

    def wait(t, carry):
        for k in range(TOP_K):
            copy(t, k).wait()
        return carry

    lax.fori_loop(0, tm, start, 0)
    lax.fori_loop(0, tm, wait, 0)


def _dispatch(dest, h2, xs_init, tm):
    t, d = h2.shape
    return pl.pallas_call(
        _dispatch_kernel,
        grid=(t // tm,),
        in_specs=[pl.BlockSpec((TOP_K, tm), lambda i: (0, i), memory_space=pltpu.SMEM),
                  pl.BlockSpec((tm, d), lambda i: (i, 0)),
                  pl.BlockSpec(memory_space=pl.ANY)],
        out_specs=pl.BlockSpec(memory_space=pl.ANY),
        out_shape=jax.ShapeDtypeStruct(xs_init.shape, xs_init.dtype),
        scratch_shapes=[pltpu.SemaphoreType.DMA],
        input_output_aliases={2: 0},
        compiler_params=_params(("arbitrary",), 32),
    )(dest, h2, xs_init)


def _expert_kernel(be_ref, nu_ref, x_ref, wg_ref, wu_ref, wd_ref, o_ref, wgb, wub, wdb):
    nb = pl.program_id(0)

    @pl.when(nb < nu_ref[0])
    def _():
        prev = be_ref[jnp.maximum(nb - 1, 0)]

        @pl.when((nb == 0) | (be_ref[nb] != prev))
        def _():
            wgb[...] = wg_ref[0].astype(BF16)
            wub[...] = wu_ref[0].astype(BF16)
            wdb[...] = wd_ref[0].astype(BF16)

        x = x_ref[...].astype(BF16)
        g = _dot(x, wgb[...])
        u = _dot(x, wub[...])
        a = (g * _sigmoid(g)) * u
        o_ref[...] = _dot(a.astype(BF16), wdb[...])


def _experts(block_e, n_used, xs, w_gate, w_up, w_down):
    r, d = xs.shape
    ne, _, f = w_gate.shape
    bm = EXPERT_ROWS
    row_blk = lambda nb, be, nu: (jnp.minimum(nb, nu[0] - 1), 0)
    wsel = lambda nb, be, nu: (be[nb], 0, 0)
    return pl.pallas_call(
        _expert_kernel,
        grid_spec=pltpu.PrefetchScalarGridSpec(
            num_scalar_prefetch=2,
            grid=(r // bm,),
            in_specs=[pl.BlockSpec((bm, d), row_blk),
                      pl.BlockSpec((1, d, f), wsel),
                      pl.BlockSpec((1, d, f), wsel),
                      pl.BlockSpec((1, f, d), wsel)],
            out_specs=pl.BlockSpec((bm, d), row_blk),
            scratch_shapes=[pltpu.VMEM((d, f), BF16), pltpu.VMEM((d, f), BF16), pltpu.VMEM((f, d), BF16)],
        ),
        out_shape=jax.ShapeDtypeStruct((r, d), F32),
        compiler_params=_params(("arbitrary",), 56),
    )(block_e, n_used, xs, w_gate, w_up, w_down)


def _shared_kernel(h_ref, wg_ref, wu_ref, wd_ref, o_ref):
    x = h_ref[...].astype(BF16)
    g = _dot(x, wg_ref[...])
    u = _dot(x, wu_ref[...])
    a = (g * _sigmoid(g)) * u
    o_ref[...] = _dot(a.astype(BF16), wd_ref[...])


def _shared_expert(h2, wg, wu, wd, bm):
    t, d = h2.shape
    f = wg.shape[1]
    return pl.pallas_call(
        _shared_kernel,
        grid=(t // bm,),
        in_specs=[pl.BlockSpec((bm, d), lambda i: (i, 0)),
                  pl.BlockSpec((d, f), lambda i: (0, 0)),
                  pl.BlockSpec((d, f), lambda i: (0, 0)),
                  pl.BlockSpec((f, d), lambda i: (0, 0))],
        out_specs=pl.BlockSpec((bm, d), lambda i: (i, 0)),
        out_shape=jax.ShapeDtypeStruct((t, d), F32),
        compiler_params=_params(("arbitrary",), 48),
    )(h2, wg, wu, wd)


def _combine_kernel(dest_ref, w_ref, x1_ref, sh_ref, gt_ref, fg_ref, ys_ref, o_ref, buf, sem):
    tm = x1_ref.shape[0]

    def copy(t, k):
        return pltpu.make_async_copy(ys_ref.at[pl.ds(dest_ref[k, t], 1), :],
                                     buf.at[k, pl.ds(t, 1), :], sem)

    def start(t, carry):
        for k in range(TOP_K):
            copy(t, k).start()
        return carry

    def wait(t, carry):
        for k in range(TOP_K):
            copy(t, k).wait()
        return carry

    lax.fori_loop(0, tm, start, 0)
    lax.fori_loop(0, tm, wait, 0)

    w = w_ref[...]
    routed = buf[0] * w[:, 0:1]
    for k in range(1, TOP_K):
        routed = routed + buf[k] * w[:, k:k + 1]
    y = x1_ref[...] + gt_ref[0] * (routed + sh_ref[...])
    ms = jnp.mean(y * y, axis=-1, keepdims=True)
    o_ref[...] = y * lax.rsqrt(ms + RMS_EPS) * fg_ref[...]


def _combine(dest, w_tok, x1, shared, gt2, fg, ys, seq, tm):
    t, d = x1.shape
    per_b = seq // tm
    return pl.pallas_call(
        _combine_kernel,
        grid=(t // tm,),
        in_specs=[pl.BlockSpec((TOP_K, tm), lambda i: (0, i), memory_space=pltpu.SMEM),
                  pl.BlockSpec((tm, TOP_K), lambda i: (i, 0)),
                  pl.BlockSpec((tm, d), lambda i: (i, 0)),
                  pl.BlockSpec((tm, d), lambda i: (i, 0)),
                  pl.BlockSpec((1, 1, d), lambda i: (i // per_b, 0, 0)),
                  pl.BlockSpec((1, d), lambda i: (0, 0)),
                  pl.BlockSpec(memory_space=pl.ANY)],
        out_specs=pl.BlockSpec((tm, d), lambda i: (i, 0)),
        out_shape=jax.ShapeDtypeStruct((t, d), F32),
        scratch_shapes=[pltpu.VMEM((TOP_K, tm, d), F32), pltpu.SemaphoreType.DMA],
        compiler_params=_params(("arbitrary",), 48),
    )(dest, w_tok, x1, shared, gt2, fg, ys)


def kernel(x, c, w_ada, b_ada, norm1_g, w_in, conv_w, conv_b, lru_wa, lru_ba, lru_wx, lru_bx,
           lru_lambda, hgrn_lb, hgrn_norm_g, w_out, norm2_g, w_router, router_bias,
           w_gate, w_up, w_down, ws_gate, ws_up, ws_down, final_g):
    batch, seq, d = x.shape
    t = batch * seq
    depth = w_ada.shape[0]
    d_lru = conv_w.shape[2]
    d_hgrn = d - d_lru
    heads = d_hgrn // HEAD_DIM
    ne = w_router.shape[2]
    assert d_lru % LRU_BLOCK == 0 and d_hgrn % HEAD_DIM == 0 and seq % 512 == 0 and ne % (8 * N_GROUPS) == 0
    assert depth == 1, "the final norm is fused into the last layer's combine kernel"

    lower_bounds = jnp.cumsum(jax.nn.softmax(hgrn_lb.astype(F32), axis=0), axis=0)
    c_pad = jnp.pad(c, ((0, -batch % 8), (0, 0)))
    xcur = x
    for l in range(depth):
        mod = _adaln_mod(c_pad, w_ada[l], b_ada[l][None, :])[:batch]
        sh1, sc1, gt1, sh2, sc2, gt2 = [m[:, None, :] for m in jnp.split(mod, 6, axis=-1)]

        h = _norm_mod(xcur, norm1_g[l][None, :], sc1, sh1, 512)
        proj = _matmul(h, w_in[l].astype(BF16), min(1024, t), 1024, BF16)
        y_lru = _rg_lru(proj, conv_w[l], conv_b[l][None, :], lru_wa[l].astype(BF16), lru_ba[l][None, :],
                        lru_wx[l].astype(BF16), lru_bx[l][None, :], lru_lambda[l][None, :], batch, seq, 256)
        y_hgrn = _hgrn2(proj, lower_bounds[l][None, :], hgrn_norm_g[l][None, :], batch, seq, heads,
                        2 * d_lru // HEAD_DIM, 512)
        wo = w_out[l].astype(BF16)
        x1, h2 = _out_proj(y_lru, y_hgrn, wo[:d_lru], wo[d_lru:], xcur, gt1, norm2_g[l][None, :], sc2, sh2, 512)

        idx, w_sel, rank, counts = _route(h2, w_router[l].T.astype(BF16), router_bias[l][:, None], 512)
        counts = counts[:, 0]
        bm = EXPERT_ROWS
        pcounts = (counts + bm - 1) // bm * bm
        pends = jnp.cumsum(pcounts)
        pstarts = pends - pcounts
        dest = pstarts[idx] + rank
        n_rows = -(-(t * TOP_K + ne * (bm - 1)) // bm) * bm
        n_blocks = n_rows // bm
        n_used = (pends[-1] // bm).astype(I32)
        blk = jnp.minimum(jnp.arange(n_blocks, dtype=I32), n_used - 1)
        block_e = jnp.minimum(jnp.searchsorted(pends, blk * bm, side='right'), ne - 1).astype(I32)
        xs = _dispatch(dest, h2, jnp.zeros((n_rows, d), F32), 256)
        ys = _experts(block_e, n_used[None], xs, w_gate[l], w_up[l], w_down[l])
        shared = _shared_expert(h2, ws_gate[l].astype(BF16), ws_up[l].astype(BF16), ws_down[l].astype(BF16), 512)
        out = _combine(dest, w_sel.T, x1, shared, gt2, final_g[None, :], ys, seq, 128)
        xcur = out.reshape(batch, seq, d)
    return xcur
```

```python
import functools

import jax
import jax.numpy as jnp
from jax import lax
from jax.experimental import pallas as pl
from jax.experimental.pallas import tpu as pltpu

F32 = jnp.float32
BF16 = jnp.bfloat16
I32 = jnp.int32

RMS_EPS = 1e-6
LRU_C = 8.0
LRU_BLOCK = 128
CONV_WIDTH = 4
HEAD_DIM = 128
CHUNK = 64
SUB = 16
N_GROUPS = 8
TOPK_GROUPS = 4
TOP_K = 8
ROUTED_SCALE = 2.5
EXPERT_ROWS = 256
MIB = 1024 * 1024


def _params(sem, vmem_mib):
    return pltpu.CompilerParams(dimension_semantics=sem, vmem_limit_bytes=vmem_mib * MIB)


def _sigmoid(x):
    return 1.0 / (1.0 + jnp.exp(-x))


def _dot(a, b):
    return jnp.dot(a, b, preferred_element_type=F32)


def _dot_nt(a, b):
    return lax.dot_general(a, b, (((1,), (1,)), ((), ())), preferred_element_type=F32)


def _dot_tn(a, b):
    return lax.dot_general(a, b, (((0,), (0,)), ((), ())), preferred_element_type=F32)


def _mod_kernel(c_ref, w_ref, b_ref, o_ref):
    c = c_ref[...]
    cond = c * _sigmoid(c)
    o_ref[...] = _dot(cond.astype(BF16), w_ref[...].astype(BF16)) + b_ref[...]


def _adaln_mod(c_pad, w_ada, b_ada):
    rows, d = c_pad.shape
    n = w_ada.shape[1]
    bn = 1024
    return pl.pallas_call(
        _mod_kernel,
        grid=(n // bn,),
        in_specs=[pl.BlockSpec((rows, d), lambda j: (0, 0)),
                  pl.BlockSpec((d, bn), lambda j: (0, j)),
                  pl.BlockSpec((1, bn), lambda j: (0, j))],
        out_specs=pl.BlockSpec((rows, bn), lambda j: (0, j)),
        out_shape=jax.ShapeDtypeStruct((rows, n), F32),
        compiler_params=_params(("arbitrary",), 40),
    )(c_pad, w_ada, b_ada)


def _norm_mod_kernel(x_ref, g_ref, sc_ref, sh_ref, o_ref):
    x = x_ref[0]
    ms = jnp.mean(x * x, axis=-1, keepdims=True)
    y = x * lax.rsqrt(ms + RMS_EPS) * g_ref[...]
    o_ref[...] = (y * (1.0 + sc_ref[0]) + sh_ref[0]).astype(o_ref.dtype)


def _norm_mod(x, g, sc, sh, tn):
    b, l, d = x.shape
    nt = l // tn
    return pl.pallas_call(
        _norm_mod_kernel,
        grid=(b, nt),
        in_specs=[pl.BlockSpec((1, tn, d), lambda i, j: (i, j, 0)),
                  pl.BlockSpec((1, d), lambda i, j: (0, 0)),
                  pl.BlockSpec((1, 1, d), lambda i, j: (i, 0, 0)),
                  pl.BlockSpec((1, 1, d), lambda i, j: (i, 0, 0))],
        out_specs=pl.BlockSpec((tn, d), lambda i, j: (i * nt + j, 0)),
        out_shape=jax.ShapeDtypeStruct((b * l, d), BF16),
        compiler_params=_params(("arbitrary", "arbitrary"), 40),
    )(x, g, sc, sh)


def _mm_kernel(x_ref, w_ref, o_ref):
    o_ref[...] = _dot(x_ref[...], w_ref[...]).astype(o_ref.dtype)


def _matmul(x, w, bm, bn, out_dtype):
    m, k = x.shape
    n = w.shape[1]
    return pl.pallas_call(
        _mm_kernel,
        grid=(n // bn, m // bm),
        in_specs=[pl.BlockSpec((bm, k), lambda j, i: (i, 0)),
                  pl.BlockSpec((k, bn), lambda j, i: (0, j))],
        out_specs=pl.BlockSpec((bm, bn), lambda j, i: (i, j)),
        out_shape=jax.ShapeDtypeStruct((m, n), out_dtype),
        compiler_params=_params(("arbitrary", "arbitrary"), 48),
    )(x, w)


def _shift_rows(u, prev, k, row8):
    if k == 0:
        return u
    rolled = pltpu.roll(u, k, 0)
    head = jnp.where(row8 < k, pltpu.roll(prev, k, 0), rolled[:8])
    return jnp.concatenate([head, rolled[8:]], axis=0)


def _lru_kernel(u_ref, z_ref, cw_ref, cb_ref, wa_ref, ba_ref, wx_ref, bx_ref, lam_ref,
                o_ref, prev_ref, h_ref):
    j = pl.program_id(1)
    tt, c = u_ref.shape

    @pl.when(j == 0)
    def _():
        prev_ref[...] = jnp.zeros_like(prev_ref)
        h_ref[...] = jnp.zeros_like(h_ref)

    u = u_ref[...].astype(F32)
    prev = prev_ref[...]
    row8 = lax.broadcasted_iota(I32, (8, c), 0)
    xc = jnp.broadcast_to(cb_ref[...], (tt, c))
    for w in range(CONV_WIDTH):
        xc = xc + cw_ref[w:w + 1, :] * _shift_rows(u, prev, CONV_WIDTH - 1 - w, row8)
    prev_ref[...] = u[tt - 8:, :]

    xcb = xc.astype(BF16)
    ra, rx = [], []
    for blk in range(c // LRU_BLOCK):
        xs = xcb[:, blk * LRU_BLOCK:(blk + 1) * LRU_BLOCK]
        ra.append(_dot(xs, wa_ref[blk]))
        rx.append(_dot(xs, wx_ref[blk]))
    r = _sigmoid(jnp.concatenate(ra, axis=1) + ba_ref[...])
    ig = _sigmoid(jnp.concatenate(rx, axis=1) + bx_ref[...])

    nl = -lam_ref[...]
    softplus = jnp.maximum(nl, 0.0) + jnp.log1p(jnp.exp(-jnp.abs(nl)))
    log_a = (-LRU_C) * r * softplus
    a = jnp.exp(log_a)
    mult = jnp.sqrt(1.0 - jnp.exp(2.0 * log_a))
    row = lax.broadcasted_iota(I32, (tt, c), 0)
    mult = jnp.where((row == 0) & (j == 0), 1.0, mult)
    bv = mult * ig * xc

    s = 1
    while s < tt:
        keep = row >= s
        bv = jnp.where(keep, a * pltpu.roll(bv, s, 0) + bv, bv)
        a = jnp.where(keep, a * pltpu.roll(a, s, 0), a)
        s *= 2
    h = bv + a * h_ref[...]
    h_ref[...] = h[tt - 1:, :]

    z = z_ref[...].astype(F32)
    gelu = 0.5 * z * (1.0 + jnp.tanh(0.7978845608028654 * (z + 0.044715 * (z * z * z))))
    o_ref[...] = (h * gelu).astype(o_ref.dtype)


def _rg_lru(proj, conv_w, conv_b, wa, ba, wx, bx, lam, batch, seq, tt):
    t = proj.shape[0]
    c = conv_w.shape[1]
    nt = seq // tt
    vec = lambda: pl.BlockSpec((1, c), lambda i, j: (0, 0))
    mat = lambda: pl.BlockSpec(wa.shape, lambda i, j: (0, 0, 0))
    return pl.pallas_call(
        _lru_kernel,
        grid=(batch, nt),
        in_specs=[pl.BlockSpec((tt, c), lambda i, j: (i * nt + j, 0)),
                  pl.BlockSpec((tt, c), lambda i, j: (i * nt + j, 1)),
                  pl.BlockSpec((CONV_WIDTH, c), lambda i, j: (0, 0)),
                  vec(), mat(), vec(), mat(), vec(), vec()],
        out_specs=pl.BlockSpec((tt, c), lambda i, j: (i * nt + j, 0)),
        out_shape=jax.ShapeDtypeStruct((t, c), BF16),
        scratch_shapes=[pltpu.VMEM((8, c), F32), pltpu.VMEM((1, c), F32)],
        compiler_params=_params(("arbitrary", "arbitrary"), 48),
    )(proj, proj, conv_w, conv_b, wa, ba, wx, bx, lam)


def _hgrn_chunk(qi, fi, v, gi, lb, ng, st, ones_b):
    n = CHUNK
    nsub = n // SUB
    q = qi * _sigmoid(qi)
    f = lb + (1.0 - lb) * _sigmoid(fi)
    row = lax.broadcasted_iota(I32, (n, HEAD_DIM), 0)
    b = jnp.log2(f)
    s = 1
    while s < n:
        b = b + jnp.where(row >= s, pltpu.roll(b, s, 0), 0.0)
        s *= 2
    c = b - jnp.log2(1.0 - f)
    vb = v.astype(BF16)
    o_state = _dot_nt((q * jnp.exp2(b)).astype(BF16), st.astype(BF16))
    refb = jnp.concatenate(
        [jnp.broadcast_to(b[i * SUB:i * SUB + 1, :], (SUB, HEAD_DIM)) for i in range(nsub)], axis=0)
    qt = (q * jnp.exp2(b - refb)).astype(BF16)
    row8 = lax.broadcasted_iota(I32, (8, HEAD_DIM), 0)
    outs = []
    for i in range(nsub):
        lo = i * SUB
        acc = o_state[lo:lo + SUB, :]
        if i > 0:
            kt = jnp.exp2(b[lo:lo + 1, :] - c[:lo, :]).astype(BF16)
            sc = _dot_nt(qt[lo:lo + SUB, :], kt)
            acc = acc + _dot(sc.astype(BF16), vb[:lo, :])
        q0, q1 = q[lo:lo + 8, :], q[lo + 8:lo + SUB, :]
        b0, b1 = b[lo:lo + 8, :], b[lo + 8:lo + SUB, :]
        parts = []
        for s_ in range(SUB):
            cs = c[lo + s_:lo + s_ + 1, :]
            p1 = q1 * jnp.exp2(b1 - cs)
            if s_ < 8:
                parts.append(jnp.where(row8 >= s_, q0 * jnp.exp2(b0 - cs), 0.0))
                parts.append(p1)
            else:
                parts.append(jnp.where(row8 >= s_ - 8, p1, 0.0))
        red = _dot(jnp.concatenate(parts, axis=0).astype(BF16), ones_b)
        acc0 = jnp.zeros((8, HEAD_DIM), F32)
        acc1 = jnp.zeros((8, HEAD_DIM), F32)
        for s_ in range(SUB):
            vs = v[lo + s_:lo + s_ + 1, :]
            if s_ < 8:
                acc0 = acc0 + red[16 * s_:16 * s_ + 8, :] * vs
                acc1 = acc1 + red[16 * s_ + 8:16 * s_ + 16, :] * vs
            else:
                acc1 = acc1 + red[64 + 8 * s_:72 + 8 * s_, :] * vs
        outs.append(acc + jnp.concatenate([acc0, acc1], axis=0))
    o = jnp.concatenate(outs, axis=0)
    b_last = b[n - 1:n, :]
    kd = jnp.exp2(b_last - c).astype(BF16)
    st_new = st * jnp.exp2(b_last) + _dot_tn(vb, kd)
    ms = jnp.mean(o * o, axis=-1, keepdims=True)
    y = o * lax.rsqrt(ms + RMS_EPS) * ng * (gi * _sigmoid(gi))
    return y, st_new


def _hgrn_kernel(q_ref, f_ref, v_ref, g_ref, lb_ref, ng_ref, o_ref, st_ref):
    @pl.when(pl.program_id(2) == 0)
    def _():
        st_ref[...] = jnp.zeros_like(st_ref)

    ng = ng_ref[...]
    heads = q_ref.shape[1] // HEAD_DIM
    ones_b = jnp.ones((HEAD_DIM, HEAD_DIM), BF16)

    def body(ci, carry):
        r0 = pl.multiple_of(ci * CHUNK, CHUNK)
        rows = pl.ds(r0, CHUNK)
        for h in range(heads):
            lanes = slice(h * HEAD_DIM, (h + 1) * HEAD_DIM)
            y, st_new = _hgrn_chunk(q_ref[rows, lanes].astype(F32), f_ref[rows, lanes].astype(F32),
                                    v_ref[rows, lanes].astype(F32), g_ref[rows, lanes].astype(F32),
                                    lb_ref[:, lanes], ng, st_ref[h], ones_b)
            st_ref[h] = st_new
            o_ref[rows, lanes] = y.astype(o_ref.dtype)
        return carry

    lax.fori_loop(0, q_ref.shape[0] // CHUNK, body, 0)


def _hgrn2(proj, lb, norm_g, batch, seq, heads, col0, tt, hp):
    t = proj.shape[0]
    nt = seq // tt
    wid = hp * HEAD_DIM
    assert heads % hp == 0 and col0 % hp == 0
    col = lambda off: pl.BlockSpec((tt, wid), lambda i, h, j: (i * nt + j, (col0 + off * heads) // hp + h))
    return pl.pallas_call(
        _hgrn_kernel,
        grid=(batch, heads // hp, nt),
        in_specs=[col(0), col(1), col(2), col(3),
                  pl.BlockSpec((1, wid), lambda i, h, j: (0, h)),
                  pl.BlockSpec((1, HEAD_DIM), lambda i, h, j: (0, 0))],
        out_specs=pl.BlockSpec((tt, wid), lambda i, h, j: (i * nt + j, h)),
        out_shape=jax.ShapeDtypeStruct((t, heads * HEAD_DIM), BF16),
        scratch_shapes=[pltpu.VMEM((hp, HEAD_DIM, HEAD_DIM), F32)],
        compiler_params=_params(("arbitrary", "arbitrary", "arbitrary"), 32),
    )(proj, proj, proj, proj, lb, norm_g)


def _pack_halves(y):
    n = y.shape[1] // 2
    lo = lax.bitcast_convert_type(y[:, :n].astype(BF16).astype(F32), I32)
    hi = lax.bitcast_convert_type(y[:, n:].astype(BF16).astype(F32), I32)
    return (hi & -65536) | lax.shift_right_logical(lo, 16)


def _unpack_halves(p):
    lo = lax.bitcast_convert_type(lax.shift_left(p, 16), F32)
    hi = lax.bitcast_convert_type(p & -65536, F32)
    return jnp.concatenate([lo, hi], axis=1)


def _out_kernel(yl_ref, yh_ref, wl_ref, wh_ref, x_ref, gt_ref, g2_ref, sc_ref, sh_ref, x1_ref, h2_ref, h2p_ref):
    mix = _dot(yl_ref[...], wl_ref[...]) + _dot(yh_ref[...], wh_ref[...])
    x1 = x_ref[0] + gt_ref[0] * mix
    x1_ref[...] = x1
    ms = jnp.mean(x1 * x1, axis=-1, keepdims=True)
    y = x1 * lax.rsqrt(ms + RMS_EPS) * g2_ref[...]
    h2 = y * (1.0 + sc_ref[0]) + sh_ref[0]
    h2_ref[...] = h2.astype(BF16)
    h2p_ref[...] = _pack_halves(h2)


def _out_proj(yl, yh, wl, wh, x, gt1, g2, sc2, sh2, bm):
    b, l, d = x.shape
    t = b * l
    nt = l // bm
    cl, ch = yl.shape[1], yh.shape[1]
    bvec = lambda: pl.BlockSpec((1, 1, d), lambda i, j: (i, 0, 0))
    return pl.pallas_call(
        _out_kernel,
        grid=(b, nt),
        in_specs=[pl.BlockSpec((bm, cl), lambda i, j: (i * nt + j, 0)),
                  pl.BlockSpec((bm, ch), lambda i, j: (i * nt + j, 0)),
                  pl.BlockSpec((cl, d), lambda i, j: (0, 0)),
                  pl.BlockSpec((ch, d), lambda i, j: (0, 0)),
                  pl.BlockSpec((1, bm, d), lambda i, j: (i, j, 0)),
                  bvec(),
                  pl.BlockSpec((1, d), lambda i, j: (0, 0)),
                  bvec(), bvec()],
        out_specs=[pl.BlockSpec((bm, d), lambda i, j: (i * nt + j, 0)),
                   pl.BlockSpec((bm, d), lambda i, j: (i * nt + j, 0)),
                   pl.BlockSpec((bm, d // 2), lambda i, j: (i * nt + j, 0))],
        out_shape=[jax.ShapeDtypeStruct((t, d), F32), jax.ShapeDtypeStruct((t, d), BF16),
                   jax.ShapeDtypeStruct((t, d // 2), I32)],
        compiler_params=_params(("arbitrary", "arbitrary"), 56),
    )(yl, yh, wl, wh, x, gt1, g2, sc2, sh2)


def _route_kernel(h_ref, wr_ref, bias_ref, idx_ref, w_ref, rank_ref, cnt_ref, carry_ref):
    step = pl.program_id(0)
    tm = h_ref.shape[0]
    ne = wr_ref.shape[0]
    gsz = ne // N_GROUPS
    neg = -jnp.inf

    @pl.when(step == 0)
    def _():
        carry_ref[...] = jnp.zeros_like(carry_ref)

    logits = _dot_nt(wr_ref[...], h_ref[...])
    scores = _sigmoid(logits)
    sel = scores + bias_ref[...]

    sel3 = sel.reshape(N_GROUPS, gsz, tm)
    pos3 = lax.broadcasted_iota(I32, (N_GROUPS, gsz, tm), 1)
    m1 = jnp.max(sel3, axis=1, keepdims=True)
    i1 = jnp.min(jnp.where(sel3 == m1, pos3, gsz), axis=1, keepdims=True)
    m2 = jnp.max(jnp.where(pos3 == i1, neg, sel3), axis=1, keepdims=True)
    gs = (m1 + m2).reshape(N_GROUPS, tm)

    gidx = lax.broadcasted_iota(I32, (N_GROUPS, tm), 0)
    beaten = jnp.zeros((N_GROUPS, tm), I32)
    for gp in range(N_GROUPS):
        other = gs[gp:gp + 1, :]
        beats = (other > gs) | ((other == gs) & (gp < gidx))
        beaten = beaten + beats.astype(I32)
    gkeep = (beaten < TOPK_GROUPS).reshape(N_GROUPS, 1, tm)
    cur = jnp.where(gkeep, sel3, neg).reshape(ne, tm)

    eidx = lax.broadcasted_iota(I32, (ne, tm), 0)
    picked = jnp.zeros((ne, tm), jnp.bool_)
    idx_rows, w_rows = [], []
    for _ in range(TOP_K):
        m = jnp.max(cur, axis=0, keepdims=True)
        ik = jnp.min(jnp.where(cur == m, eidx, ne), axis=0, keepdims=True)
        hit = eidx == ik
        w_rows.append(jnp.sum(jnp.where(hit, scores, 0.0), axis=0, keepdims=True))
        idx_rows.append(ik)
        cur = jnp.where(hit, neg, cur)
        picked = picked | hit
    w = jnp.concatenate(w_rows, axis=0)
    idx_ref[...] = jnp.concatenate(idx_rows, axis=0)
    w_ref[...] = w / jnp.sum(w, axis=0, keepdims=True) * ROUTED_SCALE

    pf = picked.astype(F32)
    ta = lax.broadcasted_iota(I32, (tm, tm), 0)
    tb = lax.broadcasted_iota(I32, (tm, tm), 1)
    before = (ta < tb).astype(BF16)
    cnt = _dot(pf.astype(BF16), before) + carry_ref[...]
    rank_rows = [jnp.sum(jnp.where(eidx == ik, cnt, 0.0), axis=0, keepdims=True) for ik in idx_rows]
    rank_ref[...] = jnp.concatenate(rank_rows, axis=0).astype(I32)
    total = carry_ref[...] + jnp.sum(pf, axis=1, keepdims=True)
    carry_ref[...] = total
    cnt_ref[...] = jnp.broadcast_to(total, cnt_ref.shape).astype(I32)


def _route(h2, wr_t, bias_col, tm):
    t, d = h2.shape
    ne = wr_t.shape[0]
    row = lambda: pl.BlockSpec((TOP_K, tm), lambda i: (0, i))
    return pl.pallas_call(
        _route_kernel,
        grid=(t // tm,),
        in_specs=[pl.BlockSpec((tm, d), lambda i: (i, 0)),
                  pl.BlockSpec((ne, d), lambda i: (0, 0)),
                  pl.BlockSpec((ne, 1), lambda i: (0, 0))],
        out_specs=[row(), row(), row(), pl.BlockSpec((ne, 128), lambda i: (0, 0))],
        out_shape=[jax.ShapeDtypeStruct((TOP_K, t), I32), jax.ShapeDtypeStruct((TOP_K, t), F32),
                   jax.ShapeDtypeStruct((TOP_K, t), I32), jax.ShapeDtypeStruct((ne, 128), I32)],
        scratch_shapes=[pltpu.VMEM((ne, 1), F32)],
        compiler_params=_params(("arbitrary",), 40),
    )(h2, wr_t, bias_col)


def _slots_kernel(ps_ref, idx_ref, rank_ref, o_ref):
    idx = idx_ref[...]

    def body(e, acc):
        return jnp.where(idx == e, ps_ref[e], acc)

    o_ref[...] = lax.fori_loop(0, ps_ref.shape[0], body, jnp.zeros(idx.shape, I32)) + rank_ref[...]


def _slots(pstarts, idx, rank, tm):
    k, t = idx.shape
    blk = lambda: pl.BlockSpec((k, tm), lambda i, ps: (0, i))
    return pl.pallas_call(
        _slots_kernel,
        grid_spec=pltpu.PrefetchScalarGridSpec(
            num_scalar_prefetch=1, grid=(t // tm,), in_specs=[blk(), blk()], out_specs=blk()),
        out_shape=jax.ShapeDtypeStruct((k, t), I32),
        compiler_params=_params(("arbitrary",), 32),
    )(pstarts, idx, rank)


def _pad_bits():
    bit = EXPERT_ROWS // 2
    while bit >= 8:
        yield bit
        bit //= 2


def _dispatch_kernel(cnt_ref, ps_ref, pc_ref, dest_ref, h_ref, xs_ref, zero_ref, sem, pad_sem):
    tm = h_ref.shape[0]

    def copy(t, k):
        return pltpu.make_async_copy(h_ref.at[pl.ds(t, 1), :],
                                     xs_ref.at[pl.ds(dest_ref[k, t], 1), :], sem)

    def start(t, carry):
        for k in range(TOP_K):
            copy(t, k).start(priority=k % 2)
        return carry

    def wait(t, carry):
        for k in range(TOP_K):
            copy(t, k).wait()
        return carry

    lax.fori_loop(0, tm, start, 0)

    @pl.when(pl.program_id(0) == 0)
    def _():
        zero_ref[...] = jnp.zeros_like(zero_ref)

        def pad_copy(first, rows):
            return pltpu.make_async_copy(zero_ref.at[pl.ds(0, rows), :], xs_ref.at[pl.ds(first, rows), :], pad_sem)

        def pads(e, fn):
            pad = pc_ref[e] - cnt_ref[e]
            lo = ps_ref[e] + cnt_ref[e]
            head = pad & 7
            for i in range(7):
                @pl.when(i < head)
                def _():
                    fn(pad_copy(lo + i, 1))
            body = pad - head
            for bit in _pad_bits():
                @pl.when((body & bit) != 0)
                def _():
                    fn(pad_copy(pl.multiple_of(lo + head + (body // (2 * bit)) * (2 * bit), 8), bit))

        def start_pads(e, carry):
            pads(e, lambda cp: cp.start())
            return carry

        def wait_pads(e, carry):
            pads(e, lambda cp: cp.wait())
            return carry

        lax.fori_loop(0, cnt_ref.shape[0], start_pads, 0)
        lax.fori_loop(0, cnt_ref.shape[0], wait_pads, 0)

    lax.fori_loop(0, tm, wait, 0)


def _dispatch(counts, pstarts, pcounts, dest, h2p, n_rows, tm):
    t, dp = h2p.shape
    return pl.pallas_call(
        _dispatch_kernel,
        grid_spec=pltpu.PrefetchScalarGridSpec(
            num_scalar_prefetch=3,
            grid=(t // tm,),
            in_specs=[pl.BlockSpec((TOP_K, tm), lambda i, *_: (0, i), memory_space=pltpu.SMEM),
                      pl.BlockSpec((tm, dp), lambda i, *_: (i, 0))],
            out_specs=pl.BlockSpec(memory_space=pl.ANY),
            scratch_shapes=[pltpu.VMEM((EXPERT_ROWS // 2, dp), I32),
                            pltpu.SemaphoreType.DMA, pltpu.SemaphoreType.DMA],
        ),
        out_shape=jax.ShapeDtypeStruct((n_rows, dp), I32),
        compiler_params=_params(("arbitrary",), 32),
    )(counts, pstarts, pcounts, dest, h2p)


def _expert_kernel(be_ref, nu_ref, first_ref, slot_ref, next_ref, x_ref, wg_hbm, wu_hbm, wd_hbm, o_ref,
                   wgf, wuf, wdf, wgb, wub, wdb, sem):
    nb = pl.program_id(0)

    def fetch(e, s):
        return (pltpu.make_async_copy(wg_hbm.at[e], wgf.at[s], sem.at[s, 0]),
                pltpu.make_async_copy(wu_hbm.at[e], wuf.at[s], sem.at[s, 1]),
                pltpu.make_async_copy(wd_hbm.at[e], wdf.at[s], sem.at[s, 2]))

    @pl.when(nb == 0)
    def _():
        for cp in fetch(be_ref[0], 0):
            cp.start()

    @pl.when(first_ref[nb] == 1)
    def _():
        s = slot_ref[nb]
        for cp in fetch(be_ref[nb], s):
            cp.wait()

        @pl.when(next_ref[nb] >= 0)
        def _():
            for cp in fetch(next_ref[nb], 1 - s):
                cp.start()

        wgb[...] = wgf[s].astype(BF16)
        wub[...] = wuf[s].astype(BF16)
        wdb[...] = wdf[s].astype(BF16)

    @pl.when(nb < nu_ref[0])
    def _():
        x = _unpack_halves(x_ref[...]).astype(BF16)
        g = _dot(x, wgb[...])
        u = _dot(x, wub[...])
        a = (g * _sigmoid(g)) * u
        o_ref[...] = _pack_halves(_dot(a.astype(BF16), wdb[...]))


def _experts(block_e, n_used, first, slot, next_e, xs, w_gate, w_up, w_down):
    r, dp = xs.shape
    ne, d, f = w_gate.shape
    bm = EXPERT_ROWS
    row_blk = lambda nb, be, nu, *_: (jnp.minimum(nb, nu[0] - 1), 0)
    return pl.pallas_call(
        _expert_kernel,
        grid_spec=pltpu.PrefetchScalarGridSpec(
            num_scalar_prefetch=5,
            grid=(r // bm,),
            in_specs=[pl.BlockSpec((bm, dp), row_blk),
                      pl.BlockSpec(memory_space=pl.ANY),
                      pl.BlockSpec(memory_space=pl.ANY),
                      pl.BlockSpec(memory_space=pl.ANY)],
            out_specs=pl.BlockSpec((bm, dp), row_blk),
            scratch_shapes=[pltpu.VMEM((2, d, f), F32), pltpu.VMEM((2, d, f), F32), pltpu.VMEM((2, f, d), F32),
                            pltpu.VMEM((d, f), BF16), pltpu.VMEM((d, f), BF16), pltpu.VMEM((f, d), BF16),
                            pltpu.SemaphoreType.DMA((2, 3))],
        ),
        out_shape=jax.ShapeDtypeStruct((r, dp), I32),
        compiler_params=_params(("arbitrary",), 58),
    )(block_e, n_used, first, slot, next_e, xs, w_gate, w_up, w_down)


def _combine_kernel(dest_ref, w_ref, x1_ref, h_ref, gt_ref, fg_ref, wg_ref, wu_ref, wd_ref, ys_ref, o_ref,
                    buf, sem):
    tm = x1_ref.shape[0]

    def copy(t, k):
        return pltpu.make_async_copy(ys_ref.at[pl.ds(dest_ref[k, t], 1), :],
                                     buf.at[k, pl.ds(t, 1), :], sem)

    def start(t, carry):
        for k in range(TOP_K):
            copy(t, k).start(priority=k % 2)
        return carry

    def wait(t, carry):
        for k in range(TOP_K):
            copy(t, k).wait()
        return carry

    lax.fori_loop(0, tm, start, 0)

    x = h_ref[...]
    g = _dot(x, wg_ref[...])
    u = _dot(x, wu_ref[...])
    acc = _dot(((g * _sigmoid(g)) * u).astype(BF16), wd_ref[...])

    lax.fori_loop(0, tm, wait, 0)

    w = w_ref[...]
    routed = _unpack_halves(buf[0]) * w[:, 0:1]
    for k in range(1, TOP_K):
        routed = routed + _unpack_halves(buf[k]) * w[:, k:k + 1]
    y = x1_ref[...] + gt_ref[0] * (routed + acc)
    ms = jnp.mean(y * y, axis=-1, keepdims=True)
    o_ref[...] = y * lax.rsqrt(ms + RMS_EPS) * fg_ref[...]


def _combine(dest, w_tok, x1, h2, gt2, fg, wg, wu, wd, ys, seq, tm):
    t, d = x1.shape
    f = wg.shape[1]
    per_b = seq // tm
    const = lambda shape: pl.BlockSpec(shape, lambda i: (0, 0))
    return pl.pallas_call(
        _combine_kernel,
        grid=(t // tm,),
        in_specs=[pl.BlockSpec((TOP_K, tm), lambda i: (0, i), memory_space=pltpu.SMEM),
                  pl.BlockSpec((tm, TOP_K), lambda i: (i, 0)),
                  pl.BlockSpec((tm, d), lambda i: (i, 0)),
                  pl.BlockSpec((tm, d), lambda i: (i, 0)),
                  pl.BlockSpec((1, 1, d), lambda i: (i // per_b, 0, 0)),
                  const((1, d)), const((d, f)), const((d, f)), const((f, d)),
                  pl.BlockSpec(memory_space=pl.ANY)],
        out_specs=pl.BlockSpec((tm, d), lambda i: (i, 0)),
        out_shape=jax.ShapeDtypeStruct((t, d), F32),
        scratch_shapes=[pltpu.VMEM((TOP_K, tm, ys.shape[1]), I32), pltpu.SemaphoreType.DMA],
        compiler_params=_params(("arbitrary",), 48),
    )(dest, w_tok, x1, h2, gt2, fg, wg, wu, wd, ys)


def _block_tables(pends, n_blocks):
    bm = EXPERT_ROWS
    n_used = (pends[-1] // bm).astype(I32)
    j = jnp.arange(n_blocks, dtype=I32)
    jc = jnp.minimum(j, n_used - 1)
    block_e = jnp.sum((pends[None, :] <= (jc * bm)[:, None]).astype(I32), axis=1)
    prev_e = jnp.concatenate([jnp.full((1,), -1, I32), block_e[:-1]])
    first = ((block_e != prev_e) & (j < n_used)).astype(I32)
    slot = (jnp.cumsum(first) - 1) % 2
    big = jnp.int32(n_blocks)
    later = lax.cummin(jnp.where(first == 1, j, big)[::-1])[::-1]
    nxt = jnp.concatenate([later[1:], big[None]])
    next_e = jnp.where(nxt < big, block_e[jnp.minimum(nxt, n_blocks - 1)], -1).astype(I32)
    return block_e, n_used[None], first, slot.astype(I32), next_e


def kernel(x, c, w_ada, b_ada, norm1_g, w_in, conv_w, conv_b, lru_wa, lru_ba, lru_wx, lru_bx,
           lru_lambda, hgrn_lb, hgrn_norm_g, w_out, norm2_g, w_router, router_bias,
           w_gate, w_up, w_down, ws_gate, ws_up, ws_down, final_g):
    batch, seq, d = x.shape
    t = batch * seq
    depth = w_ada.shape[0]
    d_lru = conv_w.shape[2]
    d_hgrn = d - d_lru
    heads = d_hgrn // HEAD_DIM
    ne = w_router.shape[2]
    assert d_lru % LRU_BLOCK == 0 and d_hgrn % HEAD_DIM == 0 and seq % 512 == 0 and ne % (8 * N_GROUPS) == 0
    assert depth == 1, "the final norm is fused into the combine kernel"
    l = 0

    lower_bounds = jnp.cumsum(jax.nn.softmax(hgrn_lb.astype(F32), axis=0), axis=0)
    c_pad = jnp.pad(c, ((0, -batch % 8), (0, 0)))
    mod = _adaln_mod(c_pad, w_ada[l], b_ada[l][None, :])[:batch]
    sh1, sc1, gt1, sh2, sc2, gt2 = [m[:, None, :] for m in jnp.split(mod, 6, axis=-1)]

    h = _norm_mod(x, norm1_g[l][None, :], sc1, sh1, 512)
    proj = _matmul(h, w_in[l].astype(BF16), min(1024, t), 1024, BF16)
    y_lru = _rg_lru(proj, conv_w[l], conv_b[l][None, :], lru_wa[l].astype(BF16), lru_ba[l][None, :],
                    lru_wx[l].astype(BF16), lru_bx[l][None, :], lru_lambda[l][None, :], batch, seq, 256)
    y_hgrn = _hgrn2(proj, lower_bounds[l][None, :], hgrn_norm_g[l][None, :], batch, seq, heads,
                    2 * d_lru // HEAD_DIM, 512, 4)
    wo = w_out[l].astype(BF16)
    x1, h2, h2p = _out_proj(y_lru, y_hgrn, wo[:d_lru], wo[d_lru:], x, gt1, norm2_g[l][None, :], sc2, sh2, 512)

    idx, w_sel, rank, counts = _route(h2, w_router[l].T.astype(BF16), router_bias[l][:, None], 512)
    counts = counts[:, 0]
    bm = EXPERT_ROWS
    pcounts = (counts + bm - 1) // bm * bm
    pends = jnp.cumsum(pcounts)
    pstarts = pends - pcounts
    n_rows = -(-(t * TOP_K + ne * (bm - 1)) // bm) * bm
    dest = _slots(pstarts, idx, rank, min(2048, t))
    xs = _dispatch(counts, pstarts, pcounts, dest, h2p, n_rows, 256)
    ys = _experts(*_block_tables(pends, n_rows // bm), xs, w_gate[l], w_up[l], w_down[l])
    out = _combine(dest, w_sel.T, x1, h2, gt2, final_g[None, :], ws_gate[l].astype(BF16),
                   ws_up[l].astype(BF16), ws_down[l].astype(BF16), ys, seq, 256)
    return out.reshape(batch, seq, d)
```

```python
import functools

import jax
import jax.numpy as jnp
from jax import lax
from jax.experimental import pallas as pl
from jax.experimental.pallas import tpu as pltpu

F32 = jnp.float32
BF16 = jnp.bfloat16
I32 = jnp.int32

RMS_EPS = 1e-6
LRU_C = 8.0
LRU_BLOCK = 128
CONV_WIDTH = 4
HEAD_DIM = 128
CHUNK = 64
SUB = 16
N_GROUPS = 8
TOPK_GROUPS = 4
TOP_K = 8
ROUTED_SCALE = 2.5
EXPERT_ROWS = 256
ROW_SUBLANES = 8
MIB = 1024 * 1024


def _params(sem, vmem_mib):
    return pltpu.CompilerParams(dimension_semantics=sem, vmem_limit_bytes=vmem_mib * MIB)


def _sigmoid(x):
    return 1.0 / (1.0 + jnp.exp(-x))


def _dot(a, b):
    return jnp.dot(a, b, preferred_element_type=F32)


def _dot_nt(a, b):
    return lax.dot_general(a, b, (((1,), (1,)), ((), ())), preferred_element_type=F32)


def _dot_tn(a, b):
    return lax.dot_general(a, b, (((0,), (0,)), ((), ())), preferred_element_type=F32)


def _mod_kernel(c_ref, w_ref, b_ref, o_ref):
    c = c_ref[...]
    cond = c * _sigmoid(c)
    o_ref[...] = _dot(cond.astype(BF16), w_ref[...].astype(BF16)) + b_ref[...]


def _adaln_mod(c_pad, w_ada, b_ada):
    rows, d = c_pad.shape
    n = w_ada.shape[1]
    bn = 1024
    return pl.pallas_call(
        _mod_kernel,
        grid=(n // bn,),
        in_specs=[pl.BlockSpec((rows, d), lambda j: (0, 0)),
                  pl.BlockSpec((d, bn), lambda j: (0, j)),
                  pl.BlockSpec((1, bn), lambda j: (0, j))],
        out_specs=pl.BlockSpec((rows, bn), lambda j: (0, j)),
        out_shape=jax.ShapeDtypeStruct((rows, n), F32),
        compiler_params=_params(("arbitrary",), 40),
    )(c_pad, w_ada, b_ada)


def _norm_mod_kernel(x_ref, g_ref, sc_ref, sh_ref, o_ref):
    x = x_ref[0]
    ms = jnp.mean(x * x, axis=-1, keepdims=True)
    y = x * lax.rsqrt(ms + RMS_EPS) * g_ref[...]
    o_ref[...] = (y * (1.0 + sc_ref[0]) + sh_ref[0]).astype(o_ref.dtype)


def _norm_mod(x, g, sc, sh, tn):
    b, l, d = x.shape
    nt = l // tn
    return pl.pallas_call(
        _norm_mod_kernel,
        grid=(b, nt),
        in_specs=[pl.BlockSpec((1, tn, d), lambda i, j: (i, j, 0)),
                  pl.BlockSpec((1, d), lambda i, j: (0, 0)),
                  pl.BlockSpec((1, 1, d), lambda i, j: (i, 0, 0)),
                  pl.BlockSpec((1, 1, d), lambda i, j: (i, 0, 0))],
        out_specs=pl.BlockSpec((tn, d), lambda i, j: (i * nt + j, 0)),
        out_shape=jax.ShapeDtypeStruct((b * l, d), BF16),
        compiler_params=_params(("arbitrary", "arbitrary"), 40),
    )(x, g, sc, sh)


def _mm_kernel(x_ref, w_ref, o_ref):
    o_ref[...] = _dot(x_ref[...], w_ref[...]).astype(o_ref.dtype)


def _matmul(x, w, bm, bn, out_dtype):
    m, k = x.shape
    n = w.shape[1]
    return pl.pallas_call(
        _mm_kernel,
        grid=(n // bn, m // bm),
        in_specs=[pl.BlockSpec((bm, k), lambda j, i: (i, 0)),
                  pl.BlockSpec((k, bn), lambda j, i: (0, j))],
        out_specs=pl.BlockSpec((bm, bn), lambda j, i: (i, j)),
        out_shape=jax.ShapeDtypeStruct((m, n), out_dtype),
        compiler_params=_params(("arbitrary", "arbitrary"), 48),
    )(x, w)


def _shift_rows(u, prev, k, row8):
    if k == 0:
        return u
    rolled = pltpu.roll(u, k, 0)
    head = jnp.where(row8 < k, pltpu.roll(prev, k, 0), rolled[:8])
    return jnp.concatenate([head, rolled[8:]], axis=0)


def _lru_kernel(u_ref, z_ref, cw_ref, cb_ref, wa_ref, ba_ref, wx_ref, bx_ref, lam_ref,
                o_ref, prev_ref, h_ref):
    j = pl.program_id(1)
    tt, c = u_ref.shape

    @pl.when(j == 0)
    def _():
        prev_ref[...] = jnp.zeros_like(prev_ref)
        h_ref[...] = jnp.zeros_like(h_ref)

    u = u_ref[...].astype(F32)
    prev = prev_ref[...]
    row8 = lax.broadcasted_iota(I32, (8, c), 0)
    xc = jnp.broadcast_to(cb_ref[...], (tt, c))
    for w in range(CONV_WIDTH):
        xc = xc + cw_ref[w:w + 1, :] * _shift_rows(u, prev, CONV_WIDTH - 1 - w, row8)
    prev_ref[...] = u[tt - 8:, :]

    xcb = xc.astype(BF16)
    ra, rx = [], []
    for blk in range(c // LRU_BLOCK):
        xs = xcb[:, blk * LRU_BLOCK:(blk + 1) * LRU_BLOCK]
        ra.append(_dot(xs, wa_ref[blk]))
        rx.append(_dot(xs, wx_ref[blk]))
    r = _sigmoid(jnp.concatenate(ra, axis=1) + ba_ref[...])
    ig = _sigmoid(jnp.concatenate(rx, axis=1) + bx_ref[...])

    nl = -lam_ref[...]
    softplus = jnp.maximum(nl, 0.0) + jnp.log1p(jnp.exp(-jnp.abs(nl)))
    log_a = (-LRU_C) * r * softplus
    a = jnp.exp(log_a)
    mult = jnp.sqrt(1.0 - jnp.exp(2.0 * log_a))
    row = lax.broadcasted_iota(I32, (tt, c), 0)
    mult = jnp.where((row == 0) & (j == 0), 1.0, mult)
    bv = mult * ig * xc

    s = 1
    while s < tt:
        keep = row >= s
        bv = jnp.where(keep, a * pltpu.roll(bv, s, 0) + bv, bv)
        a = jnp.where(keep, a * pltpu.roll(a, s, 0), a)
        s *= 2
    h = bv + a * h_ref[...]
    h_ref[...] = h[tt - 1:, :]

    z = z_ref[...].astype(F32)
    gelu = 0.5 * z * (1.0 + jnp.tanh(0.7978845608028654 * (z + 0.044715 * (z * z * z))))
    o_ref[...] = (h * gelu).astype(o_ref.dtype)


def _rg_lru(proj, conv_w, conv_b, wa, ba, wx, bx, lam, batch, seq, tt):
    t = proj.shape[0]
    c = conv_w.shape[1]
    nt = seq // tt
    vec = lambda: pl.BlockSpec((1, c), lambda i, j: (0, 0))
    mat = lambda: pl.BlockSpec(wa.shape, lambda i, j: (0, 0, 0))
    return pl.pallas_call(
        _lru_kernel,
        grid=(batch, nt),
        in_specs=[pl.BlockSpec((tt, c), lambda i, j: (i * nt + j, 0)),
                  pl.BlockSpec((tt, c), lambda i, j: (i * nt + j, 1)),
                  pl.BlockSpec((CONV_WIDTH, c), lambda i, j: (0, 0)),
                  vec(), mat(), vec(), mat(), vec(), vec()],
        out_specs=pl.BlockSpec((tt, c), lambda i, j: (i * nt + j, 0)),
        out_shape=jax.ShapeDtypeStruct((t, c), BF16),
        scratch_shapes=[pltpu.VMEM((8, c), F32), pltpu.VMEM((1, c), F32)],
        compiler_params=_params(("arbitrary", "arbitrary"), 48),
    )(proj, proj, conv_w, conv_b, wa, ba, wx, bx, lam)


def _hgrn_lanes(wid):
    return [slice(h * HEAD_DIM, (h + 1) * HEAD_DIM) for h in range(wid // HEAD_DIM)]


def _per_head(wid, fn):
    return jnp.concatenate([fn(h, ln) for h, ln in enumerate(_hgrn_lanes(wid))], axis=1)


_N_SUB = CHUNK // SUB
_STACK_ROWS = 8 * 16 + 8 * 8
_KT_OFFSET = [SUB * i * (i - 1) // 2 for i in range(_N_SUB + 1)]


def _hgrn_front_shapes(wid):
    return [pltpu.VMEM((CHUNK, wid), BF16),
            pltpu.VMEM((CHUNK, wid), BF16),
            pltpu.VMEM((CHUNK, wid), BF16),
            pltpu.VMEM((CHUNK, wid), BF16),
            pltpu.VMEM((1, wid), F32),
            pltpu.VMEM((CHUNK, wid), F32),
            pltpu.VMEM((CHUNK, wid), F32),
            pltpu.VMEM((_KT_OFFSET[_N_SUB], wid), BF16),
            pltpu.VMEM((_N_SUB * _STACK_ROWS, wid), BF16)]


def _hgrn_front(qi, fi, v, gi, lb, out):
    qe_ref, qt_ref, vb_ref, kd_ref, decay_ref, gate_ref, v_ref, kt_ref, stack_ref = out
    n, wid = qi.shape
    q = qi * _sigmoid(qi)
    f = lb + (1.0 - lb) * _sigmoid(fi)
    row = lax.broadcasted_iota(I32, (n, wid), 0)
    b = jnp.log2(f)
    s = 1
    while s < n:
        b = b + jnp.where(row >= s, pltpu.roll(b, s, 0), 0.0)
        s *= 2
    c = b - jnp.log2(1.0 - f)
    qe_ref[...] = (q * jnp.exp2(b)).astype(BF16)
    refb = jnp.concatenate(
        [jnp.broadcast_to(b[i * SUB:i * SUB + 1, :], (SUB, wid)) for i in range(_N_SUB)], axis=0)
    qt_ref[...] = (q * jnp.exp2(b - refb)).astype(BF16)
    vb_ref[...] = v.astype(BF16)
    v_ref[...] = v
    gate_ref[...] = gi * _sigmoid(gi)
    b_last = b[n - 1:n, :]
    kd_ref[...] = jnp.exp2(b_last - c).astype(BF16)
    decay_ref[...] = jnp.exp2(b_last)
    row8 = lax.broadcasted_iota(I32, (8, wid), 0)
    for i in range(_N_SUB):
        lo = i * SUB
        if i > 0:
            kt_ref[_KT_OFFSET[i]:_KT_OFFSET[i + 1], :] = jnp.exp2(b[lo:lo + 1, :] - c[:lo, :]).astype(BF16)
        q0, q1 = q[lo:lo + 8, :], q[lo + 8:lo + SUB, :]
        b0, b1 = b[lo:lo + 8, :], b[lo + 8:lo + SUB, :]
        base = i * _STACK_ROWS
        pending = None
        for s_ in range(SUB):
            cs = c[lo + s_:lo + s_ + 1, :]
            p1 = q1 * jnp.exp2(b1 - cs)
            if s_ < 8:
                pair = [jnp.where(row8 >= s_, q0 * jnp.exp2(b0 - cs), 0.0), p1]
                at = base + 16 * s_
            elif pending is None:
                pending = jnp.where(row8 >= s_ - 8, p1, 0.0)
                continue
            else:
                pair = [pending, jnp.where(row8 >= s_ - 8, p1, 0.0)]
                pending = None
                at = base + 64 + 8 * (s_ - 1)
            stack_ref[at:at + 16, :] = jnp.concatenate(pair, axis=0).astype(BF16)


def _hgrn_products(front, states, ones_b):
    qe_ref, qt_ref, _, _, _, _, _, kt_ref, stack_ref = front
    wid = qe_ref.shape[1]
    lanes = _hgrn_lanes(wid)
    o_state = _per_head(wid, lambda h, ln: _dot_nt(qe_ref[:, ln], states[h].astype(BF16)))
    scores = [[_dot_nt(qt_ref[i * SUB:(i + 1) * SUB, ln], kt_ref[_KT_OFFSET[i]:_KT_OFFSET[i + 1], ln]).astype(BF16)
               for ln in lanes] for i in range(1, _N_SUB)]
    reds = [_per_head(wid, lambda h, ln: _dot(stack_ref[i * _STACK_ROWS:(i + 1) * _STACK_ROWS, ln], ones_b))
            for i in range(_N_SUB)]
    return o_state, scores, reds


def _hgrn_finish(front, products, states, ng):
    _, _, vb_ref, kd_ref, decay_ref, gate_ref, v_ref, _, _ = front
    o_state, scores, reds = products
    n, wid = vb_ref.shape
    outs = []
    for i in range(_N_SUB):
        lo = i * SUB
        acc = o_state[lo:lo + SUB, :]
        if i > 0:
            sc = scores[i - 1]
            acc = acc + _per_head(wid, lambda h, ln: _dot(sc[h], vb_ref[:lo, ln]))
        red = reds[i]
        acc0 = jnp.zeros((8, wid), F32)
        acc1 = jnp.zeros((8, wid), F32)
        for s_ in range(SUB):
            vs = v_ref[lo + s_:lo + s_ + 1, :]
            if s_ < 8:
                acc0 = acc0 + red[16 * s_:16 * s_ + 8, :] * vs
                acc1 = acc1 + red[16 * s_ + 8:16 * s_ + 16, :] * vs
            else:
                acc1 = acc1 + red[64 + 8 * s_:72 + 8 * s_, :] * vs
        outs.append(acc + jnp.concatenate([acc0, acc1], axis=0))
    o = jnp.concatenate(outs, axis=0)
    decay = decay_ref[...]
    new_states = [states[h] * decay[:, ln] + _dot_tn(vb_ref[:, ln], kd_ref[:, ln])
                  for h, ln in enumerate(_hgrn_lanes(wid))]
    oo = o * o
    scale = _per_head(wid, lambda h, ln: jnp.broadcast_to(
        lax.rsqrt(jnp.mean(oo[:, ln], axis=-1, keepdims=True) + RMS_EPS), (n, HEAD_DIM)) * ng)
    return o * scale * gate_ref[...], new_states


def _hgrn_kernel(q_ref, f_ref, v_ref, g_ref, lb_ref, ng_ref, o_ref, st_ref, *front_refs):
    @pl.when(pl.program_id(2) == 0)
    def _():
        st_ref[...] = jnp.zeros_like(st_ref)

    lb = lb_ref[...]
    ng = ng_ref[...]
    heads = q_ref.shape[1] // HEAD_DIM
    n_chunks = q_ref.shape[0] // CHUNK
    ones_b = jnp.ones((HEAD_DIM, HEAD_DIM), BF16)
    half = len(front_refs) // 2
    fronts = (front_refs[:half], front_refs[half:])

    def chunk_rows(ci):
        return pl.ds(pl.multiple_of(ci * CHUNK, CHUNK), CHUNK)

    def front(ci, out):
        rows = chunk_rows(ci)
        _hgrn_front(q_ref[rows, :].astype(F32), f_ref[rows, :].astype(F32),
                    v_ref[rows, :].astype(F32), g_ref[rows, :].astype(F32), lb, out)

    def back(ci, cur, nxt):
        states = [st_ref[h] for h in range(heads)]
        products = _hgrn_products(cur, states, ones_b)
        front(jnp.minimum(ci + 1, n_chunks - 1), nxt)
        y, new_states = _hgrn_finish(cur, products, states, ng)
        for h in range(heads):
            st_ref[h] = new_states[h]
        o_ref[chunk_rows(ci), :] = y.astype(o_ref.dtype)

    def body(pair, carry):
        back(2 * pair, fronts[0], fronts[1])
        back(2 * pair + 1, fronts[1], fronts[0])
        return carry

    front(0, fronts[0])
    lax.fori_loop(0, n_chunks // 2, body, 0)


def _hgrn2(proj, lb, norm_g, batch, seq, heads, col0, tt, hp):
    t = proj.shape[0]
    nt = seq // tt
    wid = hp * HEAD_DIM
    assert heads % hp == 0 and col0 % hp == 0
    col = lambda off: pl.BlockSpec((tt, wid), lambda i, h, j: (i * nt + j, (col0 + off * heads) // hp + h))
    return pl.pallas_call(
        _hgrn_kernel,
        grid=(batch, heads // hp, nt),
        in_specs=[col(0), col(1), col(2), col(3),
                  pl.BlockSpec((1, wid), lambda i, h, j: (0, h)),
                  pl.BlockSpec((1, HEAD_DIM), lambda i, h, j: (0, 0))],
        out_specs=pl.BlockSpec((tt, wid), lambda i, h, j: (i * nt + j, h)),
        out_shape=jax.ShapeDtypeStruct((t, heads * HEAD_DIM), BF16),
        scratch_shapes=[pltpu.VMEM((hp, HEAD_DIM, HEAD_DIM), F32)] + 2 * _hgrn_front_shapes(wid),
        compiler_params=_params(("arbitrary", "arbitrary", "arbitrary"), 48),
    )(proj, proj, proj, proj, lb, norm_g)


def _pack_halves(y):
    n = y.shape[1] // 2
    lo = lax.bitcast_convert_type(y[:, :n].astype(BF16).astype(F32), I32)
    hi = lax.bitcast_convert_type(y[:, n:].astype(BF16).astype(F32), I32)
    return (hi & -65536) | lax.shift_right_logical(lo, 16)


def _unpack_halves(p):
    lo = lax.bitcast_convert_type(lax.shift_left(p, 16), F32)
    hi = lax.bitcast_convert_type(p & -65536, F32)
    return jnp.concatenate([lo, hi], axis=1)


def _store_row_tiles(ref, packed):
    m = packed.shape[0]
    for s in range(ROW_SUBLANES):
        ref[pl.ds(s, m, stride=ROW_SUBLANES), :] = packed[:, s * 128:(s + 1) * 128]


def _load_row_tiles(ref, m):
    return jnp.concatenate([ref[pl.ds(s, m, stride=ROW_SUBLANES), :] for s in range(ROW_SUBLANES)], axis=1)


def _out_kernel(yl_ref, yh_ref, wl_ref, wh_ref, x_ref, gt_ref, g2_ref, sc_ref, sh_ref, x1_ref, h2_ref, h2p_ref):
    mix = _dot(yl_ref[...], wl_ref[...]) + _dot(yh_ref[...], wh_ref[...])
    x1 = x_ref[0] + gt_ref[0] * mix
    x1_ref[...] = x1
    ms = jnp.mean(x1 * x1, axis=-1, keepdims=True)
    y = x1 * lax.rsqrt(ms + RMS_EPS) * g2_ref[...]
    h2 = y * (1.0 + sc_ref[0]) + sh_ref[0]
    h2_ref[...] = h2.astype(BF16)
    _store_row_tiles(h2p_ref, _pack_halves(h2))


def _out_proj(yl, yh, wl, wh, x, gt1, g2, sc2, sh2, bm):
    b, l, d = x.shape
    t = b * l
    nt = l // bm
    cl, ch = yl.shape[1], yh.shape[1]
    bvec = lambda: pl.BlockSpec((1, 1, d), lambda i, j: (i, 0, 0))
    return pl.pallas_call(
        _out_kernel,
        grid=(b, nt),
        in_specs=[pl.BlockSpec((bm, cl), lambda i, j: (i * nt + j, 0)),
                  pl.BlockSpec((bm, ch), lambda i, j: (i * nt + j, 0)),
                  pl.BlockSpec((cl, d), lambda i, j: (0, 0)),
                  pl.BlockSpec((ch, d), lambda i, j: (0, 0)),
                  pl.BlockSpec((1, bm, d), lambda i, j: (i, j, 0)),
                  bvec(),
                  pl.BlockSpec((1, d), lambda i, j: (0, 0)),
                  bvec(), bvec()],
        out_specs=[pl.BlockSpec((bm, d), lambda i, j: (i * nt + j, 0)),
                   pl.BlockSpec((bm, d), lambda i, j: (i * nt + j, 0)),
                   pl.BlockSpec((bm * ROW_SUBLANES, 128), lambda i, j: (i * nt + j, 0))],
        out_shape=[jax.ShapeDtypeStruct((t, d), F32), jax.ShapeDtypeStruct((t, d), BF16),
                   jax.ShapeDtypeStruct((t * ROW_SUBLANES, 128), I32)],
        compiler_params=_params(("arbitrary", "arbitrary"), 56),
    )(yl, yh, wl, wh, x, gt1, g2, sc2, sh2)


def _route_kernel(h_ref, wr_ref, bias_ref, idx_ref, w_ref, rank_ref, cnt_ref, carry_ref):
    step = pl.program_id(0)
    tm = h_ref.shape[0]
    ne = wr_ref.shape[0]
    gsz = ne // N_GROUPS
    neg = -jnp.inf

    @pl.when(step == 0)
    def _():
        carry_ref[...] = jnp.zeros_like(carry_ref)

    logits = _dot_nt(wr_ref[...], h_ref[...])
    scores = _sigmoid(logits)
    sel = scores + bias_ref[...]

    sel3 = sel.reshape(N_GROUPS, gsz, tm)
    pos3 = lax.broadcasted_iota(I32, (N_GROUPS, gsz, tm), 1)
    m1 = jnp.max(sel3, axis=1, keepdims=True)
    i1 = jnp.min(jnp.where(sel3 == m1, pos3, gsz), axis=1, keepdims=True)
    m2 = jnp.max(jnp.where(pos3 == i1, neg, sel3), axis=1, keepdims=True)
    gs = (m1 + m2).reshape(N_GROUPS, tm)

    gidx = lax.broadcasted_iota(I32, (N_GROUPS, tm), 0)
    beaten = jnp.zeros((N_GROUPS, tm), I32)
    for gp in range(N_GROUPS):
        other = gs[gp:gp + 1, :]
        beats = (other > gs) | ((other == gs) & (gp < gidx))
        beaten = beaten + beats.astype(I32)
    gkeep = (beaten < TOPK_GROUPS).reshape(N_GROUPS, 1, tm)
    cur = jnp.where(gkeep, sel3, neg).reshape(ne, tm)

    eidx = lax.broadcasted_iota(I32, (ne, tm), 0)
    picked = jnp.zeros((ne, tm), jnp.bool_)
    idx_rows, w_rows = [], []
    for _ in range(TOP_K):
        m = jnp.max(cur, axis=0, keepdims=True)
        ik = jnp.min(jnp.where(cur == m, eidx, ne), axis=0, keepdims=True)
        hit = eidx == ik
        w_rows.append(jnp.sum(jnp.where(hit, scores, 0.0), axis=0, keepdims=True))
        idx_rows.append(ik)
        cur = jnp.where(hit, neg, cur)
        picked = picked | hit
    w = jnp.concatenate(w_rows, axis=0)
    idx_ref[...] = jnp.concatenate(idx_rows, axis=0)
    w_ref[...] = w / jnp.sum(w, axis=0, keepdims=True) * ROUTED_SCALE

    pf = picked.astype(F32)
    ta = lax.broadcasted_iota(I32, (tm, tm), 0)
    tb = lax.broadcasted_iota(I32, (tm, tm), 1)
    before = (ta < tb).astype(BF16)
    cnt = _dot(pf.astype(BF16), before) + carry_ref[...]
    rank_rows = [jnp.sum(jnp.where(eidx == ik, cnt, 0.0), axis=0, keepdims=True) for ik in idx_rows]
    rank_ref[...] = jnp.concatenate(rank_rows, axis=0).astype(I32)
    total = carry_ref[...] + jnp.sum(pf, axis=1, keepdims=True)
    carry_ref[...] = total
    cnt_ref[...] = jnp.broadcast_to(total, cnt_ref.shape).astype(I32)


def _route(h2, wr_t, bias_col, tm):
    t, d = h2.shape
    ne = wr_t.shape[0]
    row = lambda: pl.BlockSpec((TOP_K, tm), lambda i: (0, i))
    return pl.pallas_call(
        _route_kernel,
        grid=(t // tm,),
        in_specs=[pl.BlockSpec((tm, d), lambda i: (i, 0)),
                  pl.BlockSpec((ne, d), lambda i: (0, 0)),
                  pl.BlockSpec((ne, 1), lambda i: (0, 0))],
        out_specs=[row(), row(), row(), pl.BlockSpec((ne, 128), lambda i: (0, 0))],
        out_shape=[jax.ShapeDtypeStruct((TOP_K, t), I32), jax.ShapeDtypeStruct((TOP_K, t), F32),
                   jax.ShapeDtypeStruct((TOP_K, t), I32), jax.ShapeDtypeStruct((ne, 128), I32)],
        scratch_shapes=[pltpu.VMEM((ne, 1), F32)],
        compiler_params=_params(("arbitrary",), 40),
    )(h2, wr_t, bias_col)


def _slots_kernel(ps_ref, idx_ref, rank_ref, o_ref):
    idx = idx_ref[...]

    def body(e, acc):
        return jnp.where(idx == e, ps_ref[e], acc)

    row = lax.fori_loop(0, ps_ref.shape[0], body, jnp.zeros(idx.shape, I32)) + rank_ref[...]
    o_ref[...] = row * ROW_SUBLANES


def _slots(pstarts, idx, rank, tm):
    k, t = idx.shape
    blk = lambda: pl.BlockSpec((k, tm), lambda i, ps: (0, i))
    return pl.pallas_call(
        _slots_kernel,
        grid_spec=pltpu.PrefetchScalarGridSpec(
            num_scalar_prefetch=1, grid=(t // tm,), in_specs=[blk(), blk()], out_specs=blk()),
        out_shape=jax.ShapeDtypeStruct((k, t), I32),
        compiler_params=_params(("arbitrary",), 32),
    )(pstarts, idx, rank)


def _pad_bits():
    bit = EXPERT_ROWS // 2
    while bit >= 1:
        yield bit
        bit //= 2


def _row_tile(ref, first_sublane_row, rows=1):
    return ref.at[pl.ds(pl.multiple_of(first_sublane_row, ROW_SUBLANES), rows * ROW_SUBLANES), :]


def _dispatch_kernel(cnt_ref, ps_ref, pc_ref, dest_ref, h_ref, xs_ref, zero_ref, sem, pad_sem):
    tm = h_ref.shape[0] // ROW_SUBLANES

    def copy(t, k):
        return pltpu.make_async_copy(_row_tile(h_ref, t * ROW_SUBLANES), _row_tile(xs_ref, dest_ref[k, t]), sem)

    def start(t, carry):
        for k in range(TOP_K):
            copy(t, k).start(priority=k % 2)
        return carry

    def wait(t, carry):
        for k in range(TOP_K):
            copy(t, k).wait()
        return carry

    lax.fori_loop(0, tm, start, 0)

    @pl.when(pl.program_id(0) == 0)
    def _():
        zero_ref[...] = jnp.zeros_like(zero_ref)

        def pad_copy(first, rows):
            return pltpu.make_async_copy(_row_tile(zero_ref, 0, rows),
                                         _row_tile(xs_ref, first * ROW_SUBLANES, rows), pad_sem)

        def pads(e, fn):
            pad = pc_ref[e] - cnt_ref[e]
            lo = ps_ref[e] + cnt_ref[e]
            for bit in _pad_bits():
                @pl.when((pad & bit) != 0)
                def _():
                    fn(pad_copy(lo + (pad // (2 * bit)) * (2 * bit), bit))

        def start_pads(e, carry):
            pads(e, lambda cp: cp.start())
            return carry

        def wait_pads(e, carry):
            pads(e, lambda cp: cp.wait())
            return carry

        lax.fori_loop(0, cnt_ref.shape[0], start_pads, 0)
        lax.fori_loop(0, cnt_ref.shape[0], wait_pads, 0)

    lax.fori_loop(0, tm, wait, 0)


def _dispatch(counts, pstarts, pcounts, dest, h2p, n_rows, tm):
    t = h2p.shape[0] // ROW_SUBLANES
    return pl.pallas_call(
        _dispatch_kernel,
        grid_spec=pltpu.PrefetchScalarGridSpec(
            num_scalar_prefetch=3,
            grid=(t // tm,),
            in_specs=[pl.BlockSpec((TOP_K, tm), lambda i, *_: (0, i), memory_space=pltpu.SMEM),
                      pl.BlockSpec((tm * ROW_SUBLANES, 128), lambda i, *_: (i, 0))],
            out_specs=pl.BlockSpec(memory_space=pl.ANY),
            scratch_shapes=[pltpu.VMEM((EXPERT_ROWS // 2 * ROW_SUBLANES, 128), I32),
                            pltpu.SemaphoreType.DMA, pltpu.SemaphoreType.DMA],
        ),
        out_shape=jax.ShapeDtypeStruct((n_rows * ROW_SUBLANES, 128), I32),
        compiler_params=_params(("arbitrary",), 32),
    )(counts, pstarts, pcounts, dest, h2p)


def _expert_kernel(be_ref, nu_ref, first_ref, slot_ref, next_ref, x_ref, wg_hbm, wu_hbm, wd_hbm, o_ref,
                   wgf, wuf, wdf, wgb, wub, wdb, sem):
    nb = pl.program_id(0)

    def fetch(e, s):
        return (pltpu.make_async_copy(wg_hbm.at[e], wgf.at[s], sem.at[s, 0]),
                pltpu.make_async_copy(wu_hbm.at[e], wuf.at[s], sem.at[s, 1]),
                pltpu.make_async_copy(wd_hbm.at[e], wdf.at[s], sem.at[s, 2]))

    @pl.when(nb == 0)
    def _():
        for cp in fetch(be_ref[0], 0):
            cp.start()

    @pl.when(first_ref[nb] == 1)
    def _():
        s = slot_ref[nb]
        for cp in fetch(be_ref[nb], s):
            cp.wait()

        @pl.when(next_ref[nb] >= 0)
        def _():
            for cp in fetch(next_ref[nb], 1 - s):
                cp.start()

        wgb[...] = wgf[s].astype(BF16)
        wub[...] = wuf[s].astype(BF16)
        wdb[...] = wdf[s].astype(BF16)

    @pl.when(nb < nu_ref[0])
    def _():
        x = _unpack_halves(_load_row_tiles(x_ref, EXPERT_ROWS)).astype(BF16)
        g = _dot(x, wgb[...])
        u = _dot(x, wub[...])
        a = (g * _sigmoid(g)) * u
        _store_row_tiles(o_ref, _pack_halves(_dot(a.astype(BF16), wdb[...])))


def _experts(block_e, n_used, first, slot, next_e, xs, w_gate, w_up, w_down):
    r, dp = xs.shape
    ne, d, f = w_gate.shape
    bm = EXPERT_ROWS * ROW_SUBLANES
    row_blk = lambda nb, be, nu, *_: (jnp.minimum(nb, nu[0] - 1), 0)
    return pl.pallas_call(
        _expert_kernel,
        grid_spec=pltpu.PrefetchScalarGridSpec(
            num_scalar_prefetch=5,
            grid=(r // bm,),
            in_specs=[pl.BlockSpec((bm, dp), row_blk),
                      pl.BlockSpec(memory_space=pl.ANY),
                      pl.BlockSpec(memory_space=pl.ANY),
                      pl.BlockSpec(memory_space=pl.ANY)],
            out_specs=pl.BlockSpec((bm, dp), row_blk),
            scratch_shapes=[pltpu.VMEM((2, d, f), F32), pltpu.VMEM((2, d, f), F32), pltpu.VMEM((2, f, d), F32),
                            pltpu.VMEM((d, f), BF16), pltpu.VMEM((d, f), BF16), pltpu.VMEM((f, d), BF16),
                            pltpu.SemaphoreType.DMA((2, 3))],
        ),
        out_shape=jax.ShapeDtypeStruct((r, dp), I32),
        compiler_params=_params(("arbitrary",), 58),
    )(block_e, n_used, first, slot, next_e, xs, w_gate, w_up, w_down)


def _combine_kernel(dest_ref, dnext_ref, w_ref, x1_ref, h_ref, gt_ref, fg_ref, wg_ref, wu_ref, wd_ref, ys_ref,
                    o_ref, buf, routed_ref, sem):
    i = pl.program_id(0)
    last = pl.num_programs(0) - 1
    tm = x1_ref.shape[0]
    slot = i % 2

    def copy(idx_ref, s, t, k):
        return pltpu.make_async_copy(_row_tile(ys_ref, idx_ref[k, t]),
                                     _row_tile(buf.at[s, k], t * ROW_SUBLANES), sem.at[s])

    def start_token(idx_ref, s, t):
        for k in range(TOP_K):
            copy(idx_ref, s, t, k).start(priority=k % 2)

    def wait_tile(s):
        def body(t, carry):
            for k in range(TOP_K):
                copy(dest_ref, s, t, k).wait()
            return carry
        lax.fori_loop(0, tm, body, 0)

    @pl.when(i == 0)
    def _():
        def body(t, carry):
            start_token(dest_ref, 0, t)
            return carry
        lax.fori_loop(0, tm, body, 0)

    wait_tile(slot)

    def group(g, carry):
        for r in range(8):
            start_token(dnext_ref, 1 - slot, g * 8 + r)
        base = pl.multiple_of(g * (8 * ROW_SUBLANES), 8 * ROW_SUBLANES)
        w8 = w_ref[pl.ds(pl.multiple_of(g * 8, 8), 8), :]
        lo = [None] * ROW_SUBLANES
        hi = [None] * ROW_SUBLANES
        for k in range(TOP_K):
            wk = w8[:, k:k + 1]
            for s in range(ROW_SUBLANES):
                p = buf[slot, k, pl.ds(base + s, 8, stride=ROW_SUBLANES), :]
                l = lax.bitcast_convert_type(lax.shift_left(p, 16), F32) * wk
                h = lax.bitcast_convert_type(p & -65536, F32) * wk
                lo[s] = l if k == 0 else lo[s] + l
                hi[s] = h if k == 0 else hi[s] + h
        routed_ref[pl.ds(pl.multiple_of(g * 8, 8), 8), :] = jnp.concatenate(lo + hi, axis=1)
        return carry

    lax.fori_loop(0, tm // 8, group, 0)

    x = h_ref[...]
    g = _dot(x, wg_ref[...])
    u = _dot(x, wu_ref[...])
    acc = _dot(((g * _sigmoid(g)) * u).astype(BF16), wd_ref[...])
    y = x1_ref[...] + gt_ref[0] * (routed_ref[...] + acc)
    ms = jnp.mean(y * y, axis=-1, keepdims=True)
    o_ref[...] = y * lax.rsqrt(ms + RMS_EPS) * fg_ref[...]

    @pl.when(i == last)
    def _():
        wait_tile(1 - slot)


def _combine(dest, w_tok, x1, h2, gt2, fg, wg, wu, wd, ys, seq, tm):
    t, d = x1.shape
    f = wg.shape[1]
    nt = t // tm
    per_b = seq // tm
    const = lambda shape: pl.BlockSpec(shape, lambda i: (0, 0))
    return pl.pallas_call(
        _combine_kernel,
        grid=(nt,),
        in_specs=[pl.BlockSpec((TOP_K, tm), lambda i: (0, i), memory_space=pltpu.SMEM),
                  pl.BlockSpec((TOP_K, tm), lambda i: (0, jnp.minimum(i + 1, nt - 1)), memory_space=pltpu.SMEM),
                  pl.BlockSpec((tm, TOP_K), lambda i: (i, 0)),
                  pl.BlockSpec((tm, d), lambda i: (i, 0)),
                  pl.BlockSpec((tm, d), lambda i: (i, 0)),
                  pl.BlockSpec((1, 1, d), lambda i: (i // per_b, 0, 0)),
                  const((1, d)), const((d, f)), const((d, f)), const((f, d)),
                  pl.BlockSpec(memory_space=pl.ANY)],
        out_specs=pl.BlockSpec((tm, d), lambda i: (i, 0)),
        out_shape=jax.ShapeDtypeStruct((t, d), F32),
        scratch_shapes=[pltpu.VMEM((2, TOP_K, tm * ROW_SUBLANES, 128), I32), pltpu.VMEM((tm, d), F32),
                        pltpu.SemaphoreType.DMA((2,))],
        compiler_params=_params(("arbitrary",), 56),
    )(dest, dest, w_tok, x1, h2, gt2, fg, wg, wu, wd, ys)


def _block_tables(pends, n_blocks):
    bm = EXPERT_ROWS
    n_used = (pends[-1] // bm).astype(I32)
    j = jnp.arange(n_blocks, dtype=I32)
    jc = jnp.minimum(j, n_used - 1)
    block_e = jnp.sum((pends[None, :] <= (jc * bm)[:, None]).astype(I32), axis=1)
    prev_e = jnp.concatenate([jnp.full((1,), -1, I32), block_e[:-1]])
    first = ((block_e != prev_e) & (j < n_used)).astype(I32)
    slot = (jnp.cumsum(first) - 1) % 2
    big = jnp.int32(n_blocks)
    later = lax.cummin(jnp.where(first == 1, j, big)[::-1])[::-1]
    nxt = jnp.concatenate([later[1:], big[None]])
    next_e = jnp.where(nxt < big, block_e[jnp.minimum(nxt, n_blocks - 1)], -1).astype(I32)
    return block_e, n_used[None], first, slot.astype(I32), next_e


def kernel(x, c, w_ada, b_ada, norm1_g, w_in, conv_w, conv_b, lru_wa, lru_ba, lru_wx, lru_bx,
           lru_lambda, hgrn_lb, hgrn_norm_g, w_out, norm2_g, w_router, router_bias,
           w_gate, w_up, w_down, ws_gate, ws_up, ws_down, final_g):
    batch, seq, d = x.shape
    t = batch * seq
    depth = w_ada.shape[0]
    d_lru = conv_w.shape[2]
    d_hgrn = d - d_lru
    heads = d_hgrn // HEAD_DIM
    ne = w_router.shape[2]
    assert d_lru % LRU_BLOCK == 0 and d_hgrn % HEAD_DIM == 0 and seq % 512 == 0 and ne % (8 * N_GROUPS) == 0
    assert depth == 1, "the final norm is fused into the combine kernel"
    assert d == 2 * 128 * ROW_SUBLANES, "a packed row must be exactly one (ROW_SUBLANES, 128) tile"
    l = 0

    lower_bounds = jnp.cumsum(jax.nn.softmax(hgrn_lb.astype(F32), axis=0), axis=0)
    c_pad = jnp.pad(c, ((0, -batch % 8), (0, 0)))
    mod = _adaln_mod(c_pad, w_ada[l], b_ada[l][None, :])[:batch]
    sh1, sc1, gt1, sh2, sc2, gt2 = [m[:, None, :] for m in jnp.split(mod, 6, axis=-1)]

    h = _norm_mod(x, norm1_g[l][None, :], sc1, sh1, 512)
    proj = _matmul(h, w_in[l].astype(BF16), min(1024, t), 1024, BF16)
    y_lru = _rg_lru(proj, conv_w[l], conv_b[l][None, :], lru_wa[l].astype(BF16), lru_ba[l][None, :],
                    lru_wx[l].astype(BF16), lru_bx[l][None, :], lru_lambda[l][None, :], batch, seq, 256)
    y_hgrn = _hgrn2(proj, lower_bounds[l][None, :], hgrn_norm_g[l][None, :], batch, seq, heads,
                    2 * d_lru // HEAD_DIM, min(seq, 2048), 4)
    wo = w_out[l].astype(BF16)
    x1, h2, h2p = _out_proj(y_lru, y_hgrn, wo[:d_lru], wo[d_lru:], x, gt1, norm2_g[l][None, :], sc2, sh2, 512)

    idx, w_sel, rank, counts = _route(h2, w_router[l].T.astype(BF16), router_bias[l][:, None], 512)
    counts = counts[:, 0]
    bm = EXPERT_ROWS
    pcounts = (counts + bm - 1) // bm * bm
    pends = jnp.cumsum(pcounts)
    pstarts = pends - pcounts
    n_rows = -(-(t * TOP_K + ne * (bm - 1)) // bm) * bm
    dest = _slots(pstarts, idx, rank, min(2048, t))
    xs = _dispatch(counts, pstarts, pcounts, dest, h2p, n_rows, 256)
    ys = _experts(*_block_tables(pends, n_rows // bm), xs, w_gate[l], w_up[l], w_down[l])
    out = _combine(dest, w_sel.T, x1, h2, gt2, final_g[None, :], ws_gate[l].astype(BF16),
                   ws_up[l].astype(BF16), ws_down[l].astype(BF16), ys, seq, 256)
    return out.reshape(batch, seq, d)
```

```python
import functools

import jax
import jax.numpy as jnp
from jax import lax
from jax.experimental import pallas as pl
from jax.experimental.pallas import tpu as pltpu

F32 = jnp.float32
BF16 = jnp.bfloat16
I32 = jnp.int32

RMS_EPS = 1e-6
LRU_C = 8.0
LRU_BLOCK = 128
CONV_WIDTH = 4
HEAD_DIM = 128
CHUNK = 64
SUB = 16
N_GROUPS = 8
TOPK_GROUPS = 4
TOP_K = 8
ROUTED_SCALE = 2.5
EXPERT_ROWS = 256
ROW_SUBLANES = 8
MIB = 1024 * 1024


def _params(sem, vmem_mib):
    return pltpu.CompilerParams(dimension_semantics=sem, vmem_limit_bytes=vmem_mib * MIB)


def _sigmoid(x):
    return 0.5 * jnp.tanh(0.5 * x) + 0.5


def _dot(a, b):
    return jnp.dot(a, b, preferred_element_type=F32)


def _dot_nt(a, b):
    return lax.dot_general(a, b, (((1,), (1,)), ((), ())), preferred_element_type=F32)


def _dot_tn(a, b):
    return lax.dot_general(a, b, (((0,), (0,)), ((), ())), preferred_element_type=F32)


def _mod_kernel(c_ref, w_ref, b_ref, o_ref):
    c = c_ref[...]
    cond = c * _sigmoid(c)
    o_ref[...] = _dot(cond.astype(BF16), w_ref[...].astype(BF16)) + b_ref[...]


def _adaln_mod(c_pad, w_ada, b_ada):
    rows, d = c_pad.shape
    n = w_ada.shape[1]
    bn = 1024
    return pl.pallas_call(
        _mod_kernel,
        grid=(n // bn,),
        in_specs=[pl.BlockSpec((rows, d), lambda j: (0, 0)),
                  pl.BlockSpec((d, bn), lambda j: (0, j)),
                  pl.BlockSpec((1, bn), lambda j: (0, j))],
        out_specs=pl.BlockSpec((rows, bn), lambda j: (0, j)),
        out_shape=jax.ShapeDtypeStruct((rows, n), F32),
        compiler_params=_params(("arbitrary",), 40),
    )(c_pad, w_ada, b_ada)


def _norm_mod_kernel(x_ref, g_ref, sc_ref, sh_ref, o_ref):
    x = x_ref[0]
    ms = jnp.mean(x * x, axis=-1, keepdims=True)
    y = x * lax.rsqrt(ms + RMS_EPS) * g_ref[...]
    o_ref[...] = (y * (1.0 + sc_ref[0]) + sh_ref[0]).astype(o_ref.dtype)


def _norm_mod(x, g, sc, sh, tn):
    b, l, d = x.shape
    nt = l // tn
    return pl.pallas_call(
        _norm_mod_kernel,
        grid=(b, nt),
        in_specs=[pl.BlockSpec((1, tn, d), lambda i, j: (i, j, 0)),
                  pl.BlockSpec((1, d), lambda i, j: (0, 0)),
                  pl.BlockSpec((1, 1, d), lambda i, j: (i, 0, 0)),
                  pl.BlockSpec((1, 1, d), lambda i, j: (i, 0, 0))],
        out_specs=pl.BlockSpec((tn, d), lambda i, j: (i * nt + j, 0)),
        out_shape=jax.ShapeDtypeStruct((b * l, d), BF16),
        compiler_params=_params(("arbitrary", "arbitrary"), 40),
    )(x, g, sc, sh)


def _mm_kernel(x_ref, w_ref, o_ref):
    o_ref[...] = _dot(x_ref[...], w_ref[...]).astype(o_ref.dtype)


def _matmul(x, w, bm, bn, out_dtype):
    m, k = x.shape
    n = w.shape[1]
    return pl.pallas_call(
        _mm_kernel,
        grid=(n // bn, m // bm),
        in_specs=[pl.BlockSpec((bm, k), lambda j, i: (i, 0)),
                  pl.BlockSpec((k, bn), lambda j, i: (0, j))],
        out_specs=pl.BlockSpec((bm, bn), lambda j, i: (i, j)),
        out_shape=jax.ShapeDtypeStruct((m, n), out_dtype),
        compiler_params=_params(("arbitrary", "arbitrary"), 48),
    )(x, w)


def _shift_rows(u, prev, k):
    if k == 0:
        return u
    tt, c = u.shape
    rot = pltpu.roll(u.reshape(tt // 8, 8, c), k, 1)
    before = jnp.concatenate([pltpu.roll(prev, k, 0)[None], rot[:-1]], axis=0)
    r = lax.broadcasted_iota(I32, rot.shape, 1)
    return jnp.where(r < k, before, rot).reshape(tt, c)


def _lru_kernel(u_ref, z_ref, cw_ref, cb_ref, wa_ref, ba_ref, wx_ref, bx_ref, lam_ref,
                o_ref, prev_ref, h_ref):
    j = pl.program_id(1)
    tt, c = u_ref.shape

    @pl.when(j == 0)
    def _():
        prev_ref[...] = jnp.zeros_like(prev_ref)
        h_ref[...] = jnp.zeros_like(h_ref)

    u = u_ref[...].astype(F32)
    prev = prev_ref[...]
    xc = jnp.broadcast_to(cb_ref[...], (tt, c))
    for w in range(CONV_WIDTH):
        xc = xc + cw_ref[w:w + 1, :] * _shift_rows(u, prev, CONV_WIDTH - 1 - w)
    prev_ref[...] = u[tt - 8:, :]

    xcb = xc.astype(BF16)
    ra, rx = [], []
    for blk in range(c // LRU_BLOCK):
        xs = xcb[:, blk * LRU_BLOCK:(blk + 1) * LRU_BLOCK]
        ra.append(_dot(xs, wa_ref[blk]))
        rx.append(_dot(xs, wx_ref[blk]))
    r = _sigmoid(jnp.concatenate(ra, axis=1) + ba_ref[...])
    ig = _sigmoid(jnp.concatenate(rx, axis=1) + bx_ref[...])

    nl = -lam_ref[...]
    softplus = jnp.maximum(nl, 0.0) + jnp.log1p(jnp.exp(-jnp.abs(nl)))
    log_a = (-LRU_C) * r * softplus
    a = jnp.exp(log_a)
    mult = jnp.sqrt(1.0 - jnp.exp(2.0 * log_a))
    row = lax.broadcasted_iota(I32, (tt, c), 0)
    mult = jnp.where((row == 0) & (j == 0), 1.0, mult)
    bv = mult * ig * xc

    a3 = a.reshape(tt // 8, 8, c)
    b3 = bv.reshape(tt // 8, 8, c)
    in_group = lax.broadcasted_iota(I32, a3.shape, 1)
    s = 1
    while s < 8:
        keep = in_group >= s
        b3 = jnp.where(keep, a3 * pltpu.roll(b3, s, 1) + b3, b3)
        a3 = jnp.where(keep, a3 * pltpu.roll(a3, s, 1), a3)
        s *= 2
    carry = h_ref[...]
    groups = []
    for gi in range(tt // 8):
        hg = b3[gi] + a3[gi] * carry
        carry = hg[7:8, :]
        groups.append(hg)
    h = jnp.concatenate(groups, axis=0)
    h_ref[...] = carry

    z = z_ref[...].astype(F32)
    gelu = 0.5 * z * (1.0 + jnp.tanh(0.7978845608028654 * (z + 0.044715 * (z * z * z))))
    o_ref[...] = (h * gelu).astype(o_ref.dtype)


def _rg_lru(proj, conv_w, conv_b, wa, ba, wx, bx, lam, batch, seq, tt):
    t = proj.shape[0]
    c = conv_w.shape[1]
    nt = seq // tt
    vec = lambda: pl.BlockSpec((1, c), lambda i, j: (0, 0))
    mat = lambda: pl.BlockSpec(wa.shape, lambda i, j: (0, 0, 0))
    return pl.pallas_call(
        _lru_kernel,
        grid=(batch, nt),
        in_specs=[pl.BlockSpec((tt, c), lambda i, j: (i * nt + j, 0)),
                  pl.BlockSpec((tt, c), lambda i, j: (i * nt + j, 1)),
                  pl.BlockSpec((CONV_WIDTH, c), lambda i, j: (0, 0)),
                  vec(), mat(), vec(), mat(), vec(), vec()],
        out_specs=pl.BlockSpec((tt, c), lambda i, j: (i * nt + j, 0)),
        out_shape=jax.ShapeDtypeStruct((t, c), BF16),
        scratch_shapes=[pltpu.VMEM((8, c), F32), pltpu.VMEM((1, c), F32)],
        compiler_params=_params(("arbitrary", "arbitrary"), 48),
    )(proj, proj, conv_w, conv_b, wa, ba, wx, bx, lam)


def _hgrn_lanes(wid):
    return [slice(h * HEAD_DIM, (h + 1) * HEAD_DIM) for h in range(wid // HEAD_DIM)]


def _per_head(wid, fn):
    return jnp.concatenate([fn(h, ln) for h, ln in enumerate(_hgrn_lanes(wid))], axis=1)


_N_SUB = CHUNK // SUB
_STACK_ROWS = 8 * 16 + 8 * 8
_KT_OFFSET = [SUB * i * (i - 1) // 2 for i in range(_N_SUB + 1)]


def _hgrn_front_shapes(wid):
    return [pltpu.VMEM((CHUNK, wid), BF16),
            pltpu.VMEM((CHUNK, wid), BF16),
            pltpu.VMEM((CHUNK, wid), BF16),
            pltpu.VMEM((CHUNK, wid), BF16),
            pltpu.VMEM((1, wid), F32),
            pltpu.VMEM((CHUNK, wid), F32),
            pltpu.VMEM((CHUNK, wid), F32),
            pltpu.VMEM((_KT_OFFSET[_N_SUB], wid), BF16),
            pltpu.VMEM((_N_SUB * _STACK_ROWS, wid), BF16)]


def _hgrn_front(qi, fi, v, gi, lb, out):
    qe_ref, qt_ref, vb_ref, kd_ref, decay_ref, gate_ref, v_ref, kt_ref, stack_ref = out
    n, wid = qi.shape
    q = qi * _sigmoid(qi)
    f = lb + (1.0 - lb) * _sigmoid(fi)
    row = lax.broadcasted_iota(I32, (n, wid), 0)
    b = jnp.log2(f)
    s = 1
    while s < n:
        b = b + jnp.where(row >= s, pltpu.roll(b, s, 0), 0.0)
        s *= 2
    c = b - jnp.log2(1.0 - f)
    qe_ref[...] = (q * jnp.exp2(b)).astype(BF16)
    refb = jnp.concatenate(
        [jnp.broadcast_to(b[i * SUB:i * SUB + 1, :], (SUB, wid)) for i in range(_N_SUB)], axis=0)
    qt_ref[...] = (q * jnp.exp2(b - refb)).astype(BF16)
    vb_ref[...] = v.astype(BF16)
    v_ref[...] = v
    gate_ref[...] = gi * _sigmoid(gi)
    b_last = b[n - 1:n, :]
    kd_ref[...] = jnp.exp2(b_last - c).astype(BF16)
    decay_ref[...] = jnp.exp2(b_last)
    row8 = lax.broadcasted_iota(I32, (8, wid), 0)
    for i in range(_N_SUB):
        lo = i * SUB
        if i > 0:
            kt_ref[_KT_OFFSET[i]:_KT_OFFSET[i + 1], :] = jnp.exp2(b[lo:lo + 1, :] - c[:lo, :]).astype(BF16)
        q0, q1 = q[lo:lo + 8, :], q[lo + 8:lo + SUB, :]
        b0, b1 = b[lo:lo + 8, :], b[lo + 8:lo + SUB, :]
        base = i * _STACK_ROWS
        pending = None
        for s_ in range(SUB):
            cs = c[lo + s_:lo + s_ + 1, :]
            p1 = q1 * jnp.exp2(b1 - cs)
            if s_ < 8:
                pair = [jnp.where(row8 >= s_, q0 * jnp.exp2(b0 - cs), 0.0), p1]
                at = base + 16 * s_
            elif pending is None:
                pending = jnp.where(row8 >= s_ - 8, p1, 0.0)
                continue
            else:
                pair = [pending, jnp.where(row8 >= s_ - 8, p1, 0.0)]
                pending = None
                at = base + 64 + 8 * (s_ - 1)
            stack_ref[at:at + 16, :] = jnp.concatenate(pair, axis=0).astype(BF16)


def _hgrn_products(front, states, ones_b):
    qe_ref, qt_ref, _, _, _, _, _, kt_ref, stack_ref = front
    wid = qe_ref.shape[1]
    lanes = _hgrn_lanes(wid)
    o_state = _per_head(wid, lambda h, ln: _dot_nt(qe_ref[:, ln], states[h].astype(BF16)))
    scores = [[_dot_nt(qt_ref[i * SUB:(i + 1) * SUB, ln], kt_ref[_KT_OFFSET[i]:_KT_OFFSET[i + 1], ln]).astype(BF16)
               for ln in lanes] for i in range(1, _N_SUB)]
    reds = [_per_head(wid, lambda h, ln: _dot(stack_ref[i * _STACK_ROWS:(i + 1) * _STACK_ROWS, ln], ones_b))
            for i in range(_N_SUB)]
    return o_state, scores, reds


def _hgrn_finish(front, products, states, ng):
    _, _, vb_ref, kd_ref, decay_ref, gate_ref, v_ref, _, _ = front
    o_state, scores, reds = products
    n, wid = vb_ref.shape
    outs = []
    for i in range(_N_SUB):
        lo = i * SUB
        acc = o_state[lo:lo + SUB, :]
        if i > 0:
            sc = scores[i - 1]
            acc = acc + _per_head(wid, lambda h, ln: _dot(sc[h], vb_ref[:lo, ln]))
        red = reds[i]
        acc0 = jnp.zeros((8, wid), F32)
        acc1 = jnp.zeros((8, wid), F32)
        for s_ in range(SUB):
            vs = v_ref[lo + s_:lo + s_ + 1, :]
            if s_ < 8:
                acc0 = acc0 + red[16 * s_:16 * s_ + 8, :] * vs
                acc1 = acc1 + red[16 * s_ + 8:16 * s_ + 16, :] * vs
            else:
                acc1 = acc1 + red[64 + 8 * s_:72 + 8 * s_, :] * vs
        outs.append(acc + jnp.concatenate([acc0, acc1], axis=0))
    o = jnp.concatenate(outs, axis=0)
    decay = decay_ref[...]
    new_states = [states[h] * decay[:, ln] + _dot_tn(vb_ref[:, ln], kd_ref[:, ln])
                  for h, ln in enumerate(_hgrn_lanes(wid))]
    oo = o * o
    scale = _per_head(wid, lambda h, ln: jnp.broadcast_to(
        lax.rsqrt(jnp.mean(oo[:, ln], axis=-1, keepdims=True) + RMS_EPS), (n, HEAD_DIM)) * ng)
    return o * scale * gate_ref[...], new_states


def _hgrn_kernel(q_ref, f_ref, v_ref, g_ref, lb_ref, ng_ref, o_ref, st_ref, *front_refs):
    @pl.when(pl.program_id(2) == 0)
    def _():
        st_ref[...] = jnp.zeros_like(st_ref)

    lb = lb_ref[...]
    ng = ng_ref[...]
    heads = q_ref.shape[1] // HEAD_DIM
    n_chunks = q_ref.shape[0] // CHUNK
    ones_b = jnp.ones((HEAD_DIM, HEAD_DIM), BF16)
    half = len(front_refs) // 2
    fronts = (front_refs[:half], front_refs[half:])

    def chunk_rows(ci):
        return pl.ds(pl.multiple_of(ci * CHUNK, CHUNK), CHUNK)

    def front(ci, out):
        rows = chunk_rows(ci)
        _hgrn_front(q_ref[rows, :].astype(F32), f_ref[rows, :].astype(F32),
                    v_ref[rows, :].astype(F32), g_ref[rows, :].astype(F32), lb, out)

    def back(ci, cur, nxt):
        states = [st_ref[h] for h in range(heads)]
        products = _hgrn_products(cur, states, ones_b)
        front(jnp.minimum(ci + 1, n_chunks - 1), nxt)
        y, new_states = _hgrn_finish(cur, products, states, ng)
        for h in range(heads):
            st_ref[h] = new_states[h]
        o_ref[chunk_rows(ci), :] = y.astype(o_ref.dtype)

    def body(pair, carry):
        back(2 * pair, fronts[0], fronts[1])
        back(2 * pair + 1, fronts[1], fronts[0])
        return carry

    front(0, fronts[0])
    lax.fori_loop(0, n_chunks // 2, body, 0)


def _hgrn2(proj, lb, norm_g, batch, seq, heads, col0, tt, hp):
    t = proj.shape[0]
    nt = seq // tt
    wid = hp * HEAD_DIM
    assert heads % hp == 0 and col0 % hp == 0
    col = lambda off: pl.BlockSpec((tt, wid), lambda i, h, j: (i * nt + j, (col0 + off * heads) // hp + h))
    return pl.pallas_call(
        _hgrn_kernel,
        grid=(batch, heads // hp, nt),
        in_specs=[col(0), col(1), col(2), col(3),
                  pl.BlockSpec((1, wid), lambda i, h, j: (0, h)),
                  pl.BlockSpec((1, HEAD_DIM), lambda i, h, j: (0, 0))],
        out_specs=pl.BlockSpec((tt, wid), lambda i, h, j: (i * nt + j, h)),
        out_shape=jax.ShapeDtypeStruct((t, heads * HEAD_DIM), BF16),
        scratch_shapes=[pltpu.VMEM((hp, HEAD_DIM, HEAD_DIM), F32)] + 2 * _hgrn_front_shapes(wid),
        compiler_params=_params(("arbitrary", "arbitrary", "arbitrary"), 48),
    )(proj, proj, proj, proj, lb, norm_g)


def _pack_halves(y):
    n = y.shape[1] // 2
    lo = lax.bitcast_convert_type(y[:, :n].astype(BF16).astype(F32), I32)
    hi = lax.bitcast_convert_type(y[:, n:].astype(BF16).astype(F32), I32)
    return (hi & -65536) | lax.shift_right_logical(lo, 16)


def _unpack_halves(p):
    lo = lax.bitcast_convert_type(lax.shift_left(p, 16), F32)
    hi = lax.bitcast_convert_type(p & -65536, F32)
    return jnp.concatenate([lo, hi], axis=1)


def _store_row_tiles(ref, packed):
    m = packed.shape[0]
    for s in range(ROW_SUBLANES):
        ref[pl.ds(s, m, stride=ROW_SUBLANES), :] = packed[:, s * 128:(s + 1) * 128]


def _load_row_tiles(ref, m):
    return jnp.concatenate([ref[pl.ds(s, m, stride=ROW_SUBLANES), :] for s in range(ROW_SUBLANES)], axis=1)


def _out_kernel(yl_ref, yh_ref, wl_ref, wh_ref, x_ref, gt_ref, g2_ref, sc_ref, sh_ref, x1_ref, h2_ref, h2p_ref):
    mix = _dot(yl_ref[...], wl_ref[...]) + _dot(yh_ref[...], wh_ref[...])
    x1 = x_ref[0] + gt_ref[0] * mix
    x1_ref[...] = x1
    ms = jnp.mean(x1 * x1, axis=-1, keepdims=True)
    y = x1 * lax.rsqrt(ms + RMS_EPS) * g2_ref[...]
    h2 = y * (1.0 + sc_ref[0]) + sh_ref[0]
    h2_ref[...] = h2.astype(BF16)
    _store_row_tiles(h2p_ref, _pack_halves(h2))


def _out_proj(yl, yh, wl, wh, x, gt1, g2, sc2, sh2, bm):
    b, l, d = x.shape
    t = b * l
    nt = l // bm
    cl, ch = yl.shape[1], yh.shape[1]
    bvec = lambda: pl.BlockSpec((1, 1, d), lambda i, j: (i, 0, 0))
    return pl.pallas_call(
        _out_kernel,
        grid=(b, nt),
        in_specs=[pl.BlockSpec((bm, cl), lambda i, j: (i * nt + j, 0)),
                  pl.BlockSpec((bm, ch), lambda i, j: (i * nt + j, 0)),
                  pl.BlockSpec((cl, d), lambda i, j: (0, 0)),
                  pl.BlockSpec((ch, d), lambda i, j: (0, 0)),
                  pl.BlockSpec((1, bm, d), lambda i, j: (i, j, 0)),
                  bvec(),
                  pl.BlockSpec((1, d), lambda i, j: (0, 0)),
                  bvec(), bvec()],
        out_specs=[pl.BlockSpec((bm, d), lambda i, j: (i * nt + j, 0)),
                   pl.BlockSpec((bm, d), lambda i, j: (i * nt + j, 0)),
                   pl.BlockSpec((bm * ROW_SUBLANES, 128), lambda i, j: (i * nt + j, 0))],
        out_shape=[jax.ShapeDtypeStruct((t, d), F32), jax.ShapeDtypeStruct((t, d), BF16),
                   jax.ShapeDtypeStruct((t * ROW_SUBLANES, 128), I32)],
        compiler_params=_params(("arbitrary", "arbitrary"), 56),
    )(yl, yh, wl, wh, x, gt1, g2, sc2, sh2)


def _route_kernel(h_ref, wr_ref, bias_ref, idx_ref, w_ref, rank_ref, cnt_ref, carry_ref):
    step = pl.program_id(0)
    tm = h_ref.shape[0]
    ne = wr_ref.shape[0]
    gsz = ne // N_GROUPS
    neg = -jnp.inf

    @pl.when(step == 0)
    def _():
        carry_ref[...] = jnp.zeros_like(carry_ref)

    logits = _dot_nt(wr_ref[...], h_ref[...])
    scores = _sigmoid(logits)
    sel = scores + bias_ref[...]

    sel3 = sel.reshape(N_GROUPS, gsz, tm)
    pos3 = lax.broadcasted_iota(I32, (N_GROUPS, gsz, tm), 1)
    m1 = jnp.max(sel3, axis=1, keepdims=True)
    i1 = jnp.min(jnp.where(sel3 == m1, pos3, gsz), axis=1, keepdims=True)
    m2 = jnp.max(jnp.where(pos3 == i1, neg, sel3), axis=1, keepdims=True)
    gs = (m1 + m2).reshape(N_GROUPS, tm)

    gidx = lax.broadcasted_iota(I32, (N_GROUPS, tm), 0)
    beaten = jnp.zeros((N_GROUPS, tm), I32)
    for gp in range(N_GROUPS):
        other = gs[gp:gp + 1, :]
        beats = (other > gs) | ((other == gs) & (gp < gidx))
        beaten = beaten + beats.astype(I32)
    gkeep = (beaten < TOPK_GROUPS).reshape(N_GROUPS, 1, tm)
    cur = jnp.where(gkeep, sel3, neg).reshape(ne, tm)

    eidx = lax.broadcasted_iota(I32, (ne, tm), 0)
    picked = jnp.zeros((ne, tm), jnp.bool_)
    idx_rows, w_rows = [], []
    for _ in range(TOP_K):
        m = jnp.max(cur, axis=0, keepdims=True)
        ik = jnp.min(jnp.where(cur == m, eidx, ne), axis=0, keepdims=True)
        hit = eidx == ik
        w_rows.append(jnp.sum(jnp.where(hit, scores, 0.0), axis=0, keepdims=True))
        idx_rows.append(ik)
        cur = jnp.where(hit, neg, cur)
        picked = picked | hit
    w = jnp.concatenate(w_rows, axis=0)
    idx_ref[...] = jnp.concatenate(idx_rows, axis=0)
    w_ref[...] = w / jnp.sum(w, axis=0, keepdims=True) * ROUTED_SCALE

    pf = picked.astype(F32)
    ta = lax.broadcasted_iota(I32, (tm, tm), 0)
    tb = lax.broadcasted_iota(I32, (tm, tm), 1)
    before = (ta < tb).astype(BF16)
    cnt = _dot(pf.astype(BF16), before) + carry_ref[...]
    rank_rows = [jnp.sum(jnp.where(eidx == ik, cnt, 0.0), axis=0, keepdims=True) for ik in idx_rows]
    rank_ref[...] = jnp.concatenate(rank_rows, axis=0).astype(I32)
    total = carry_ref[...] + jnp.sum(pf, axis=1, keepdims=True)
    carry_ref[...] = total
    cnt_ref[...] = jnp.broadcast_to(total, cnt_ref.shape).astype(I32)


def _route(h2, wr_t, bias_col, tm):
    t, d = h2.shape
    ne = wr_t.shape[0]
    row = lambda: pl.BlockSpec((TOP_K, tm), lambda i: (0, i))
    return pl.pallas_call(
        _route_kernel,
        grid=(t // tm,),
        in_specs=[pl.BlockSpec((tm, d), lambda i: (i, 0)),
                  pl.BlockSpec((ne, d), lambda i: (0, 0)),
                  pl.BlockSpec((ne, 1), lambda i: (0, 0))],
        out_specs=[row(), row(), row(), pl.BlockSpec((ne, 128), lambda i: (0, 0))],
        out_shape=[jax.ShapeDtypeStruct((TOP_K, t), I32), jax.ShapeDtypeStruct((TOP_K, t), F32),
                   jax.ShapeDtypeStruct((TOP_K, t), I32), jax.ShapeDtypeStruct((ne, 128), I32)],
        scratch_shapes=[pltpu.VMEM((ne, 1), F32)],
        compiler_params=_params(("arbitrary",), 40),
    )(h2, wr_t, bias_col)


def _slots_kernel(ps_ref, idx_ref, rank_ref, o_ref):
    idx = idx_ref[...]

    def body(e, acc):
        return jnp.where(idx == e, ps_ref[e], acc)

    row = lax.fori_loop(0, ps_ref.shape[0], body, jnp.zeros(idx.shape, I32)) + rank_ref[...]
    o_ref[...] = row * ROW_SUBLANES


def _slots(pstarts, idx, rank, tm):
    k, t = idx.shape
    blk = lambda: pl.BlockSpec((k, tm), lambda i, ps: (0, i))
    return pl.pallas_call(
        _slots_kernel,
        grid_spec=pltpu.PrefetchScalarGridSpec(
            num_scalar_prefetch=1, grid=(t // tm,), in_specs=[blk(), blk()], out_specs=blk()),
        out_shape=jax.ShapeDtypeStruct((k, t), I32),
        compiler_params=_params(("arbitrary",), 32),
    )(pstarts, idx, rank)


def _pad_bits():
    bit = EXPERT_ROWS // 2
    while bit >= 1:
        yield bit
        bit //= 2


def _row_tile(ref, first_sublane_row, rows=1):
    return ref.at[pl.ds(pl.multiple_of(first_sublane_row, ROW_SUBLANES), rows * ROW_SUBLANES), :]


def _dispatch_kernel(cnt_ref, ps_ref, pc_ref, dest_ref, h_ref, xs_ref, zero_ref, sem, pad_sem):
    tm = h_ref.shape[0] // ROW_SUBLANES

    def copy(t, k):
        return pltpu.make_async_copy(_row_tile(h_ref, t * ROW_SUBLANES), _row_tile(xs_ref, dest_ref[k, t]), sem)

    def start(t, carry):
        for k in range(TOP_K):
            copy(t, k).start(priority=k % 2)
        return carry

    def wait(t, carry):
        for k in range(TOP_K):
            copy(t, k).wait()
        return carry

    lax.fori_loop(0, tm, start, 0)

    @pl.when(pl.program_id(0) == 0)
    def _():
        zero_ref[...] = jnp.zeros_like(zero_ref)

        def pad_copy(first, rows):
            return pltpu.make_async_copy(_row_tile(zero_ref, 0, rows),
                                         _row_tile(xs_ref, first * ROW_SUBLANES, rows), pad_sem)

        def pads(e, fn):
            pad = pc_ref[e] - cnt_ref[e]
            lo = ps_ref[e] + cnt_ref[e]
            for bit in _pad_bits():
                @pl.when((pad & bit) != 0)
                def _():
                    fn(pad_copy(lo + (pad // (2 * bit)) * (2 * bit), bit))

        def start_pads(e, carry):
            pads(e, lambda cp: cp.start())
            return carry

        def wait_pads(e, carry):
            pads(e, lambda cp: cp.wait())
            return carry

        lax.fori_loop(0, cnt_ref.shape[0], start_pads, 0)
        lax.fori_loop(0, cnt_ref.shape[0], wait_pads, 0)

    lax.fori_loop(0, tm, wait, 0)


def _dispatch(counts, pstarts, pcounts, dest, h2p, n_rows, tm):
    t = h2p.shape[0] // ROW_SUBLANES
    return pl.pallas_call(
        _dispatch_kernel,
        grid_spec=pltpu.PrefetchScalarGridSpec(
            num_scalar_prefetch=3,
            grid=(t // tm,),
            in_specs=[pl.BlockSpec((TOP_K, tm), lambda i, *_: (0, i), memory_space=pltpu.SMEM),
                      pl.BlockSpec((tm * ROW_SUBLANES, 128), lambda i, *_: (i, 0))],
            out_specs=pl.BlockSpec(memory_space=pl.ANY),
            scratch_shapes=[pltpu.VMEM((EXPERT_ROWS // 2 * ROW_SUBLANES, 128), I32),
                            pltpu.SemaphoreType.DMA, pltpu.SemaphoreType.DMA],
        ),
        out_shape=jax.ShapeDtypeStruct((n_rows * ROW_SUBLANES, 128), I32),
        compiler_params=_params(("arbitrary",), 32),
    )(counts, pstarts, pcounts, dest, h2p)


def _expert_kernel(be_ref, rows_ref, nu_ref, first_ref, slot_ref, next_ref, x_ref, wg_hbm, wu_hbm, wd_hbm, o_ref,
                   wgf, wuf, wdf, wgb, wub, wdb, sem):
    nb = pl.program_id(0)
    half_f = wd_hbm.shape[1] // 2

    def fetch(e, s):
        lo, hi = pl.ds(0, half_f), pl.ds(half_f, half_f)
        return (pltpu.make_async_copy(wg_hbm.at[e], wgf.at[s], sem.at[s, 0]),
                pltpu.make_async_copy(wu_hbm.at[e], wuf.at[s], sem.at[s, 1]),
                pltpu.make_async_copy(wd_hbm.at[e, lo], wdf.at[s, lo], sem.at[s, 2]),
                pltpu.make_async_copy(wd_hbm.at[e, hi], wdf.at[s, hi], sem.at[s, 3]))

    def start(e, s):
        for i, cp in enumerate(fetch(e, s)):
            cp.start(priority=i % 2)

    @pl.when(nb == 0)
    def _():
        start(be_ref[0], 0)

    @pl.when(first_ref[nb] == 1)
    def _():
        s = slot_ref[nb]
        for cp in fetch(be_ref[nb], s):
            cp.wait()

        @pl.when(next_ref[nb] >= 0)
        def _():
            start(next_ref[nb], 1 - s)

        wgb[...] = wgf[s].astype(BF16)
        wub[...] = wuf[s].astype(BF16)
        wdb[...] = wdf[s].astype(BF16)

    def mlp(rows):
        x = _unpack_halves(_load_row_tiles(x_ref, rows)).astype(BF16)
        g = _dot(x, wgb[...])
        u = _dot(x, wub[...])
        a = (g * _sigmoid(g)) * u
        _store_row_tiles(o_ref, _pack_halves(_dot(a.astype(BF16), wdb[...])))

    half = EXPERT_ROWS // 2

    @pl.when(rows_ref[nb] > half)
    def _():
        mlp(EXPERT_ROWS)

    @pl.when((rows_ref[nb] > 0) & (rows_ref[nb] <= half))
    def _():
        mlp(half)
        o_ref[half * ROW_SUBLANES:, :] = jnp.zeros((half * ROW_SUBLANES, 128), I32)


def _experts(block_e, rows, n_used, first, slot, next_e, xs, w_gate, w_up, w_down):
    r, dp = xs.shape
    ne, d, f = w_gate.shape
    bm = EXPERT_ROWS * ROW_SUBLANES
    row_blk = lambda nb, be, rw, nu, *_: (jnp.minimum(nb, nu[0] - 1), 0)
    return pl.pallas_call(
        _expert_kernel,
        grid_spec=pltpu.PrefetchScalarGridSpec(
            num_scalar_prefetch=6,
            grid=(r // bm,),
            in_specs=[pl.BlockSpec((bm, dp), row_blk),
                      pl.BlockSpec(memory_space=pl.ANY),
                      pl.BlockSpec(memory_space=pl.ANY),
                      pl.BlockSpec(memory_space=pl.ANY)],
            out_specs=pl.BlockSpec((bm, dp), row_blk),
            scratch_shapes=[pltpu.VMEM((2, d, f), F32), pltpu.VMEM((2, d, f), F32), pltpu.VMEM((2, f, d), F32),
                            pltpu.VMEM((d, f), BF16), pltpu.VMEM((d, f), BF16), pltpu.VMEM((f, d), BF16),
                            pltpu.SemaphoreType.DMA((2, 4))],
        ),
        out_shape=jax.ShapeDtypeStruct((r, dp), I32),
        compiler_params=_params(("arbitrary",), 58),
    )(block_e, rows, n_used, first, slot, next_e, xs, w_gate, w_up, w_down)


def _combine_kernel(dest_ref, dnext_ref, w_ref, x1_ref, h_ref, gt_ref, fg_ref, wg_ref, wu_ref, wd_ref, ys_ref,
                    o_ref, buf, routed_ref, sem):
    i = pl.program_id(0)
    last = pl.num_programs(0) - 1
    tm = x1_ref.shape[0]
    slot = i % 2

    def copy(idx_ref, s, t, k):
        return pltpu.make_async_copy(_row_tile(ys_ref, idx_ref[k, t]),
                                     _row_tile(buf.at[s, k], t * ROW_SUBLANES), sem.at[s])

    def start_token(idx_ref, s, t):
        for k in range(TOP_K):
            copy(idx_ref, s, t, k).start(priority=k % 2)

    def wait_tile(s):
        def body(t, carry):
            for k in range(TOP_K):
                copy(dest_ref, s, t, k).wait()
            return carry
        lax.fori_loop(0, tm, body, 0)

    @pl.when(i == 0)
    def _():
        def body(t, carry):
            start_token(dest_ref, 0, t)
            return carry
        lax.fori_loop(0, tm, body, 0)

    wait_tile(slot)

    def group(g, carry):
        for r in range(8):
            start_token(dnext_ref, 1 - slot, g * 8 + r)
        base = pl.multiple_of(g * (8 * ROW_SUBLANES), 8 * ROW_SUBLANES)
        w8 = w_ref[pl.ds(pl.multiple_of(g * 8, 8), 8), :]
        lo = [None] * ROW_SUBLANES
        hi = [None] * ROW_SUBLANES
        for k in range(TOP_K):
            wk = w8[:, k:k + 1]
            for s in range(ROW_SUBLANES):
                p = buf[slot, k, pl.ds(base + s, 8, stride=ROW_SUBLANES), :]
                l = lax.bitcast_convert_type(lax.shift_left(p, 16), F32) * wk
                h = lax.bitcast_convert_type(p & -65536, F32) * wk
                lo[s] = l if k == 0 else lo[s] + l
                hi[s] = h if k == 0 else hi[s] + h
        routed_ref[pl.ds(pl.multiple_of(g * 8, 8), 8), :] = jnp.concatenate(lo + hi, axis=1)
        return carry

    lax.fori_loop(0, tm // 8, group, 0)

    x = h_ref[...]
    g = _dot(x, wg_ref[...])
    u = _dot(x, wu_ref[...])
    acc = _dot(((g * _sigmoid(g)) * u).astype(BF16), wd_ref[...])
    y = x1_ref[...] + gt_ref[0] * (routed_ref[...] + acc)
    ms = jnp.mean(y * y, axis=-1, keepdims=True)
    o_ref[...] = y * lax.rsqrt(ms + RMS_EPS) * fg_ref[...]

    @pl.when(i == last)
    def _():
        wait_tile(1 - slot)


def _combine(dest, w_tok, x1, h2, gt2, fg, wg, wu, wd, ys, seq, tm):
    t, d = x1.shape
    f = wg.shape[1]
    nt = t // tm
    per_b = seq // tm
    const = lambda shape: pl.BlockSpec(shape, lambda i: (0, 0))
    return pl.pallas_call(
        _combine_kernel,
        grid=(nt,),
        in_specs=[pl.BlockSpec((TOP_K, tm), lambda i: (0, i), memory_space=pltpu.SMEM),
                  pl.BlockSpec((TOP_K, tm), lambda i: (0, jnp.minimum(i + 1, nt - 1)), memory_space=pltpu.SMEM),
                  pl.BlockSpec((tm, TOP_K), lambda i: (i, 0)),
                  pl.BlockSpec((tm, d), lambda i: (i, 0)),
                  pl.BlockSpec((tm, d), lambda i: (i, 0)),
                  pl.BlockSpec((1, 1, d), lambda i: (i // per_b, 0, 0)),
                  const((1, d)), const((d, f)), const((d, f)), const((f, d)),
                  pl.BlockSpec(memory_space=pl.ANY)],
        out_specs=pl.BlockSpec((tm, d), lambda i: (i, 0)),
        out_shape=jax.ShapeDtypeStruct((t, d), F32),
        scratch_shapes=[pltpu.VMEM((2, TOP_K, tm * ROW_SUBLANES, 128), I32), pltpu.VMEM((tm, d), F32),
                        pltpu.SemaphoreType.DMA((2,))],
        compiler_params=_params(("arbitrary",), 56),
    )(dest, dest, w_tok, x1, h2, gt2, fg, wg, wu, wd, ys)


def _block_tables(pends, row_ends, n_blocks):
    bm = EXPERT_ROWS
    n_used = (pends[-1] // bm).astype(I32)
    j = jnp.arange(n_blocks, dtype=I32)
    jc = jnp.minimum(j, n_used - 1)
    block_e = jnp.sum((pends[None, :] <= (jc * bm)[:, None]).astype(I32), axis=1)
    rows = jnp.where(j < n_used, jnp.clip(row_ends[block_e] - j * bm, 0, bm), 0).astype(I32)
    prev_e = jnp.concatenate([jnp.full((1,), -1, I32), block_e[:-1]])
    first = ((block_e != prev_e) & (j < n_used)).astype(I32)
    slot = (jnp.cumsum(first) - 1) % 2
    big = jnp.int32(n_blocks)
    later = lax.cummin(jnp.where(first == 1, j, big)[::-1])[::-1]
    nxt = jnp.concatenate([later[1:], big[None]])
    next_e = jnp.where(nxt < big, block_e[jnp.minimum(nxt, n_blocks - 1)], -1).astype(I32)
    return block_e, rows, n_used[None], first, slot.astype(I32), next_e


def kernel(x, c, w_ada, b_ada, norm1_g, w_in, conv_w, conv_b, lru_wa, lru_ba, lru_wx, lru_bx,
           lru_lambda, hgrn_lb, hgrn_norm_g, w_out, norm2_g, w_router, router_bias,
           w_gate, w_up, w_down, ws_gate, ws_up, ws_down, final_g):
    batch, seq, d = x.shape
    t = batch * seq
    depth = w_ada.shape[0]
    d_lru = conv_w.shape[2]
    d_hgrn = d - d_lru
    heads = d_hgrn // HEAD_DIM
    ne = w_router.shape[2]
    assert d_lru % LRU_BLOCK == 0 and d_hgrn % HEAD_DIM == 0 and seq % 512 == 0 and ne % (8 * N_GROUPS) == 0
    assert depth == 1, "the final norm is fused into the combine kernel"
    assert d == 2 * 128 * ROW_SUBLANES, "a packed row must be exactly one (ROW_SUBLANES, 128) tile"
    l = 0

    lower_bounds = jnp.cumsum(jax.nn.softmax(hgrn_lb.astype(F32), axis=0), axis=0)
    c_pad = jnp.pad(c, ((0, -batch % 8), (0, 0)))
    mod = _adaln_mod(c_pad, w_ada[l], b_ada[l][None, :])[:batch]
    sh1, sc1, gt1, sh2, sc2, gt2 = [m[:, None, :] for m in jnp.split(mod, 6, axis=-1)]

    h = _norm_mod(x, norm1_g[l][None, :], sc1, sh1, 512)
    proj = _matmul(h, w_in[l].astype(BF16), min(1024, t), 1024, BF16)
    y_lru = _rg_lru(proj, conv_w[l], conv_b[l][None, :], lru_wa[l].astype(BF16), lru_ba[l][None, :],
                    lru_wx[l].astype(BF16), lru_bx[l][None, :], lru_lambda[l][None, :], batch, seq, 256)
    y_hgrn = _hgrn2(proj, lower_bounds[l][None, :], hgrn_norm_g[l][None, :], batch, seq, heads,
                    2 * d_lru // HEAD_DIM, min(seq, 2048), 4)
    wo = w_out[l].astype(BF16)
    x1, h2, h2p = _out_proj(y_lru, y_hgrn, wo[:d_lru], wo[d_lru:], x, gt1, norm2_g[l][None, :], sc2, sh2, 512)

    idx, w_sel, rank, counts = _route(h2, w_router[l].T.astype(BF16), router_bias[l][:, None], 512)
    counts = counts[:, 0]
    bm = EXPERT_ROWS
    pcounts = (counts + bm - 1) // bm * bm
    pends = jnp.cumsum(pcounts)
    pstarts = pends - pcounts
    n_rows = -(-(t * TOP_K + ne * (bm - 1)) // bm) * bm
    dest = _slots(pstarts, idx, rank, min(2048, t))
    xs = _dispatch(counts, pstarts, pcounts, dest, h2p, n_rows, 256)
    ys = _experts(*_block_tables(pends, pstarts + counts, n_rows // bm), xs, w_gate[l], w_up[l], w_down[l])
    out = _combine(dest, w_sel.T, x1, h2, gt2, final_g[None, :], ws_gate[l].astype(BF16),
                   ws_up[l].astype(BF16), ws_down[l].astype(BF16), ys, seq, 256)
    return out.reshape(batch, seq, d)
```

```python
import functools

import jax
import jax.numpy as jnp
from jax import lax
from jax.experimental import pallas as pl
from jax.experimental.pallas import tpu as pltpu

F32 = jnp.float32
BF16 = jnp.bfloat16
I32 = jnp.int32

RMS_EPS = 1e-6
LRU_C = 8.0
LRU_BLOCK = 128
CONV_WIDTH = 4
HEAD_DIM = 128
CHUNK = 64
SUB = 16
N_GROUPS = 8
TOPK_GROUPS = 4
TOP_K = 8
ROUTED_SCALE = 2.5
EXPERT_ROWS = 256
ROW_SUBLANES = 8
WEIGHT_SLOTS = 3
MIB = 1024 * 1024


def _params(sem, vmem_mib):
    return pltpu.CompilerParams(dimension_semantics=sem, vmem_limit_bytes=vmem_mib * MIB)


def _sigmoid(x):
    return 0.5 * jnp.tanh(0.5 * x) + 0.5


def _dot(a, b):
    return jnp.dot(a, b, preferred_element_type=F32)


def _dot_nt(a, b):
    return lax.dot_general(a, b, (((1,), (1,)), ((), ())), preferred_element_type=F32)


def _dot_tn(a, b):
    return lax.dot_general(a, b, (((0,), (0,)), ((), ())), preferred_element_type=F32)


def _mod_kernel(c_ref, w_ref, b_ref, o_ref):
    c = c_ref[...]
    cond = c * _sigmoid(c)
    o_ref[...] = _dot(cond.astype(BF16), w_ref[...].astype(BF16)) + b_ref[...]


def _adaln_mod(c_pad, w_ada, b_ada):
    rows, d = c_pad.shape
    n = w_ada.shape[1]
    bn = 1024
    return pl.pallas_call(
        _mod_kernel,
        grid=(n // bn,),
        in_specs=[pl.BlockSpec((rows, d), lambda j: (0, 0)),
                  pl.BlockSpec((d, bn), lambda j: (0, j)),
                  pl.BlockSpec((1, bn), lambda j: (0, j))],
        out_specs=pl.BlockSpec((rows, bn), lambda j: (0, j)),
        out_shape=jax.ShapeDtypeStruct((rows, n), F32),
        compiler_params=_params(("arbitrary",), 40),
    )(c_pad, w_ada, b_ada)


def _norm_mod_kernel(x_ref, g_ref, sc_ref, sh_ref, o_ref):
    x = x_ref[0]
    ms = jnp.mean(x * x, axis=-1, keepdims=True)
    y = x * lax.rsqrt(ms + RMS_EPS) * g_ref[...]
    o_ref[...] = (y * (1.0 + sc_ref[0]) + sh_ref[0]).astype(o_ref.dtype)


def _norm_mod(x, g, sc, sh, tn):
    b, l, d = x.shape
    nt = l // tn
    return pl.pallas_call(
        _norm_mod_kernel,
        grid=(b, nt),
        in_specs=[pl.BlockSpec((1, tn, d), lambda i, j: (i, j, 0)),
                  pl.BlockSpec((1, d), lambda i, j: (0, 0)),
                  pl.BlockSpec((1, 1, d), lambda i, j: (i, 0, 0)),
                  pl.BlockSpec((1, 1, d), lambda i, j: (i, 0, 0))],
        out_specs=pl.BlockSpec((tn, d), lambda i, j: (i * nt + j, 0)),
        out_shape=jax.ShapeDtypeStruct((b * l, d), BF16),
        compiler_params=_params(("arbitrary", "arbitrary"), 40),
    )(x, g, sc, sh)


def _mm_kernel(x_ref, w_ref, o_ref):
    o_ref[...] = _dot(x_ref[...], w_ref[...]).astype(o_ref.dtype)


def _matmul(x, w, bm, bn, out_dtype):
    m, k = x.shape
    n = w.shape[1]
    return pl.pallas_call(
        _mm_kernel,
        grid=(n // bn, m // bm),
        in_specs=[pl.BlockSpec((bm, k), lambda j, i: (i, 0)),
                  pl.BlockSpec((k, bn), lambda j, i: (0, j))],
        out_specs=pl.BlockSpec((bm, bn), lambda j, i: (i, j)),
        out_shape=jax.ShapeDtypeStruct((m, n), out_dtype),
        compiler_params=_params(("arbitrary", "arbitrary"), 48),
    )(x, w)


def _shift_rows(u, prev, k):
    if k == 0:
        return u
    tt, c = u.shape
    rot = pltpu.roll(u.reshape(tt // 8, 8, c), k, 1)
    before = jnp.concatenate([pltpu.roll(prev, k, 0)[None], rot[:-1]], axis=0)
    r = lax.broadcasted_iota(I32, rot.shape, 1)
    return jnp.where(r < k, before, rot).reshape(tt, c)


def _lru_kernel(u_ref, z_ref, cw_ref, cb_ref, wa_ref, ba_ref, wx_ref, bx_ref, lam_ref,
                o_ref, prev_ref, h_ref):
    j = pl.program_id(1)
    tt, c = u_ref.shape

    @pl.when(j == 0)
    def _():
        prev_ref[...] = jnp.zeros_like(prev_ref)
        h_ref[...] = jnp.zeros_like(h_ref)

    u = u_ref[...].astype(F32)
    prev = prev_ref[...]
    xc = jnp.broadcast_to(cb_ref[...], (tt, c))
    for w in range(CONV_WIDTH):
        xc = xc + cw_ref[w:w + 1, :] * _shift_rows(u, prev, CONV_WIDTH - 1 - w)
    prev_ref[...] = u[tt - 8:, :]

    xcb = xc.astype(BF16)
    ra, rx = [], []
    for blk in range(c // LRU_BLOCK):
        xs = xcb[:, blk * LRU_BLOCK:(blk + 1) * LRU_BLOCK]
        ra.append(_dot(xs, wa_ref[blk]))
        rx.append(_dot(xs, wx_ref[blk]))
    r = _sigmoid(jnp.concatenate(ra, axis=1) + ba_ref[...])
    ig = _sigmoid(jnp.concatenate(rx, axis=1) + bx_ref[...])

    nl = -lam_ref[...]
    softplus = jnp.maximum(nl, 0.0) + jnp.log1p(jnp.exp(-jnp.abs(nl)))
    log_a = (-LRU_C) * r * softplus
    a = jnp.exp(log_a)
    mult = jnp.sqrt(1.0 - jnp.exp(2.0 * log_a))
    row = lax.broadcasted_iota(I32, (tt, c), 0)
    mult = jnp.where((row == 0) & (j == 0), 1.0, mult)
    bv = mult * ig * xc

    a3 = a.reshape(tt // 8, 8, c)
    b3 = bv.reshape(tt // 8, 8, c)
    in_group = lax.broadcasted_iota(I32, a3.shape, 1)
    s = 1
    while s < 8:
        keep = in_group >= s
        b3 = jnp.where(keep, a3 * pltpu.roll(b3, s, 1) + b3, b3)
        a3 = jnp.where(keep, a3 * pltpu.roll(a3, s, 1), a3)
        s *= 2
    carry = h_ref[...]
    groups = []
    for gi in range(tt // 8):
        hg = b3[gi] + a3[gi] * carry
        carry = hg[7:8, :]
        groups.append(hg)
    h = jnp.concatenate(groups, axis=0)
    h_ref[...] = carry

    z = z_ref[...].astype(F32)
    gelu = 0.5 * z * (1.0 + jnp.tanh(0.7978845608028654 * (z + 0.044715 * (z * z * z))))
    o_ref[...] = (h * gelu).astype(o_ref.dtype)


def _rg_lru(proj, conv_w, conv_b, wa, ba, wx, bx, lam, batch, seq, tt):
    t = proj.shape[0]
    c = conv_w.shape[1]
    nt = seq // tt
    vec = lambda: pl.BlockSpec((1, c), lambda i, j: (0, 0))
    mat = lambda: pl.BlockSpec(wa.shape, lambda i, j: (0, 0, 0))
    return pl.pallas_call(
        _lru_kernel,
        grid=(batch, nt),
        in_specs=[pl.BlockSpec((tt, c), lambda i, j: (i * nt + j, 0)),
                  pl.BlockSpec((tt, c), lambda i, j: (i * nt + j, 1)),
                  pl.BlockSpec((CONV_WIDTH, c), lambda i, j: (0, 0)),
                  vec(), mat(), vec(), mat(), vec(), vec()],
        out_specs=pl.BlockSpec((tt, c), lambda i, j: (i * nt + j, 0)),
        out_shape=jax.ShapeDtypeStruct((t, c), BF16),
        scratch_shapes=[pltpu.VMEM((8, c), F32), pltpu.VMEM((1, c), F32)],
        compiler_params=_params(("arbitrary", "arbitrary"), 48),
    )(proj, proj, conv_w, conv_b, wa, ba, wx, bx, lam)


def _hgrn_lanes(wid):
    return [slice(h * HEAD_DIM, (h + 1) * HEAD_DIM) for h in range(wid // HEAD_DIM)]


def _per_head(wid, fn):
    return jnp.concatenate([fn(h, ln) for h, ln in enumerate(_hgrn_lanes(wid))], axis=1)


_N_SUB = CHUNK // SUB
_STACK_ROWS = 8 * 16 + 8 * 8
_KT_OFFSET = [SUB * i * (i - 1) // 2 for i in range(_N_SUB + 1)]


def _hgrn_front_shapes(wid):
    return [pltpu.VMEM((CHUNK, wid), BF16),
            pltpu.VMEM((CHUNK, wid), BF16),
            pltpu.VMEM((CHUNK, wid), BF16),
            pltpu.VMEM((CHUNK, wid), BF16),
            pltpu.VMEM((1, wid), F32),
            pltpu.VMEM((CHUNK, wid), F32),
            pltpu.VMEM((CHUNK, wid), F32),
            pltpu.VMEM((_KT_OFFSET[_N_SUB], wid), BF16),
            pltpu.VMEM((_N_SUB * _STACK_ROWS, wid), BF16)]


def _hgrn_front(qi, fi, v, gi, lb, out):
    qe_ref, qt_ref, vb_ref, kd_ref, decay_ref, gate_ref, v_ref, kt_ref, stack_ref = out
    n, wid = qi.shape
    q = qi * _sigmoid(qi)
    f = lb + (1.0 - lb) * _sigmoid(fi)
    row = lax.broadcasted_iota(I32, (n, wid), 0)
    b = jnp.log2(f)
    s = 1
    while s < n:
        b = b + jnp.where(row >= s, pltpu.roll(b, s, 0), 0.0)
        s *= 2
    c = b - jnp.log2(1.0 - f)
    qe_ref[...] = (q * jnp.exp2(b)).astype(BF16)
    refb = jnp.concatenate(
        [jnp.broadcast_to(b[i * SUB:i * SUB + 1, :], (SUB, wid)) for i in range(_N_SUB)], axis=0)
    qt_ref[...] = (q * jnp.exp2(b - refb)).astype(BF16)
    vb_ref[...] = v.astype(BF16)
    v_ref[...] = v
    gate_ref[...] = gi * _sigmoid(gi)
    b_last = b[n - 1:n, :]
    kd_ref[...] = jnp.exp2(b_last - c).astype(BF16)
    decay_ref[...] = jnp.exp2(b_last)
    row8 = lax.broadcasted_iota(I32, (8, wid), 0)
    for i in range(_N_SUB):
        lo = i * SUB
        if i > 0:
            kt_ref[_KT_OFFSET[i]:_KT_OFFSET[i + 1], :] = jnp.exp2(b[lo:lo + 1, :] - c[:lo, :]).astype(BF16)
        q0, q1 = q[lo:lo + 8, :], q[lo + 8:lo + SUB, :]
        b0, b1 = b[lo:lo + 8, :], b[lo + 8:lo + SUB, :]
        base = i * _STACK_ROWS
        pending = None
        for s_ in range(SUB):
            cs = c[lo + s_:lo + s_ + 1, :]
            p1 = q1 * jnp.exp2(b1 - cs)
            if s_ < 8:
                pair = [jnp.where(row8 >= s_, q0 * jnp.exp2(b0 - cs), 0.0), p1]
                at = base + 16 * s_
            elif pending is None:
                pending = jnp.where(row8 >= s_ - 8, p1, 0.0)
                continue
            else:
                pair = [pending, jnp.where(row8 >= s_ - 8, p1, 0.0)]
                pending = None
                at = base + 64 + 8 * (s_ - 1)
            stack_ref[at:at + 16, :] = jnp.concatenate(pair, axis=0).astype(BF16)


def _hgrn_products(front, states, ones_b):
    qe_ref, qt_ref, _, _, _, _, _, kt_ref, stack_ref = front
    wid = qe_ref.shape[1]
    lanes = _hgrn_lanes(wid)
    o_state = _per_head(wid, lambda h, ln: _dot_nt(qe_ref[:, ln], states[h].astype(BF16)))
    scores = [[_dot_nt(qt_ref[i * SUB:(i + 1) * SUB, ln], kt_ref[_KT_OFFSET[i]:_KT_OFFSET[i + 1], ln]).astype(BF16)
               for ln in lanes] for i in range(1, _N_SUB)]
    reds = [_per_head(wid, lambda h, ln: _dot(stack_ref[i * _STACK_ROWS:(i + 1) * _STACK_ROWS, ln], ones_b))
            for i in range(_N_SUB)]
    return o_state, scores, reds


def _hgrn_finish(front, products, states, ng):
    _, _, vb_ref, kd_ref, decay_ref, gate_ref, v_ref, _, _ = front
    o_state, scores, reds = products
    n, wid = vb_ref.shape
    outs = []
    for i in range(_N_SUB):
        lo = i * SUB
        acc = o_state[lo:lo + SUB, :]
        if i > 0:
            sc = scores[i - 1]
            acc = acc + _per_head(wid, lambda h, ln: _dot(sc[h], vb_ref[:lo, ln]))
        red = reds[i]
        acc0 = jnp.zeros((8, wid), F32)
        acc1 = jnp.zeros((8, wid), F32)
        for s_ in range(SUB):
            vs = v_ref[lo + s_:lo + s_ + 1, :]
            if s_ < 8:
                acc0 = acc0 + red[16 * s_:16 * s_ + 8, :] * vs
                acc1 = acc1 + red[16 * s_ + 8:16 * s_ + 16, :] * vs
            else:
                acc1 = acc1 + red[64 + 8 * s_:72 + 8 * s_, :] * vs
        outs.append(acc + jnp.concatenate([acc0, acc1], axis=0))
    o = jnp.concatenate(outs, axis=0)
    decay = decay_ref[...]
    new_states = [states[h] * decay[:, ln] + _dot_tn(vb_ref[:, ln], kd_ref[:, ln])
                  for h, ln in enumerate(_hgrn_lanes(wid))]
    oo = o * o
    scale = _per_head(wid, lambda h, ln: jnp.broadcast_to(
        lax.rsqrt(jnp.mean(oo[:, ln], axis=-1, keepdims=True) + RMS_EPS), (n, HEAD_DIM)) * ng)
    return o * scale * gate_ref[...], new_states


def _hgrn_kernel(q_ref, f_ref, v_ref, g_ref, lb_ref, ng_ref, o_ref, st_ref, *front_refs):
    @pl.when(pl.program_id(2) == 0)
    def _():
        st_ref[...] = jnp.zeros_like(st_ref)

    lb = lb_ref[...]
    ng = ng_ref[...]
    heads = q_ref.shape[1] // HEAD_DIM
    n_chunks = q_ref.shape[0] // CHUNK
    ones_b = jnp.ones((HEAD_DIM, HEAD_DIM), BF16)
    half = len(front_refs) // 2
    fronts = (front_refs[:half], front_refs[half:])

    def chunk_rows(ci):
        return pl.ds(pl.multiple_of(ci * CHUNK, CHUNK), CHUNK)

    def front(ci, out):
        rows = chunk_rows(ci)
        _hgrn_front(q_ref[rows, :].astype(F32), f_ref[rows, :].astype(F32),
                    v_ref[rows, :].astype(F32), g_ref[rows, :].astype(F32), lb, out)

    def back(ci, cur, nxt):
        states = [st_ref[h] for h in range(heads)]
        products = _hgrn_products(cur, states, ones_b)
        front(jnp.minimum(ci + 1, n_chunks - 1), nxt)
        y, new_states = _hgrn_finish(cur, products, states, ng)
        for h in range(heads):
            st_ref[h] = new_states[h]
        o_ref[chunk_rows(ci), :] = y.astype(o_ref.dtype)

    def body(pair, carry):
        back(2 * pair, fronts[0], fronts[1])
        back(2 * pair + 1, fronts[1], fronts[0])
        return carry

    front(0, fronts[0])
    lax.fori_loop(0, n_chunks // 2, body, 0)


def _hgrn2(proj, lb, norm_g, batch, seq, heads, col0, tt, hp):
    t = proj.shape[0]
    nt = seq // tt
    wid = hp * HEAD_DIM
    assert heads % hp == 0 and col0 % hp == 0
    col = lambda off: pl.BlockSpec((tt, wid), lambda i, h, j: (i * nt + j, (col0 + off * heads) // hp + h))
    return pl.pallas_call(
        _hgrn_kernel,
        grid=(batch, heads // hp, nt),
        in_specs=[col(0), col(1), col(2), col(3),
                  pl.BlockSpec((1, wid), lambda i, h, j: (0, h)),
                  pl.BlockSpec((1, HEAD_DIM), lambda i, h, j: (0, 0))],
        out_specs=pl.BlockSpec((tt, wid), lambda i, h, j: (i * nt + j, h)),
        out_shape=jax.ShapeDtypeStruct((t, heads * HEAD_DIM), BF16),
        scratch_shapes=[pltpu.VMEM((hp, HEAD_DIM, HEAD_DIM), F32)] + 2 * _hgrn_front_shapes(wid),
        compiler_params=_params(("arbitrary", "arbitrary", "arbitrary"), 48),
    )(proj, proj, proj, proj, lb, norm_g)


def _pack_halves(y):
    n = y.shape[1] // 2
    lo = lax.bitcast_convert_type(y[:, :n].astype(BF16).astype(F32), I32)
    hi = lax.bitcast_convert_type(y[:, n:].astype(BF16).astype(F32), I32)
    return (hi & -65536) | lax.shift_right_logical(lo, 16)


def _unpack_halves(p):
    lo = lax.bitcast_convert_type(lax.shift_left(p, 16), F32)
    hi = lax.bitcast_convert_type(p & -65536, F32)
    return jnp.concatenate([lo, hi], axis=1)


def _store_row_tiles(ref, packed):
    m = packed.shape[0]
    for s in range(ROW_SUBLANES):
        ref[pl.ds(s, m, stride=ROW_SUBLANES), :] = packed[:, s * 128:(s + 1) * 128]


def _load_row_tiles(ref, m):
    return jnp.concatenate([ref[pl.ds(s, m, stride=ROW_SUBLANES), :] for s in range(ROW_SUBLANES)], axis=1)


def _out_kernel(yl_ref, yh_ref, wl_ref, wh_ref, x_ref, gt_ref, g2_ref, sc_ref, sh_ref, x1_ref, h2_ref, h2p_ref):
    mix = _dot(yl_ref[...], wl_ref[...]) + _dot(yh_ref[...], wh_ref[...])
    x1 = x_ref[0] + gt_ref[0] * mix
    x1_ref[...] = x1
    ms = jnp.mean(x1 * x1, axis=-1, keepdims=True)
    y = x1 * lax.rsqrt(ms + RMS_EPS) * g2_ref[...]
    h2 = y * (1.0 + sc_ref[0]) + sh_ref[0]
    h2_ref[...] = h2.astype(BF16)
    _store_row_tiles(h2p_ref, _pack_halves(h2))


def _out_proj(yl, yh, wl, wh, x, gt1, g2, sc2, sh2, bm):
    b, l, d = x.shape
    t = b * l
    nt = l // bm
    cl, ch = yl.shape[1], yh.shape[1]
    bvec = lambda: pl.BlockSpec((1, 1, d), lambda i, j: (i, 0, 0))
    return pl.pallas_call(
        _out_kernel,
        grid=(b, nt),
        in_specs=[pl.BlockSpec((bm, cl), lambda i, j: (i * nt + j, 0)),
                  pl.BlockSpec((bm, ch), lambda i, j: (i * nt + j, 0)),
                  pl.BlockSpec((cl, d), lambda i, j: (0, 0)),
                  pl.BlockSpec((ch, d), lambda i, j: (0, 0)),
                  pl.BlockSpec((1, bm, d), lambda i, j: (i, j, 0)),
                  bvec(),
                  pl.BlockSpec((1, d), lambda i, j: (0, 0)),
                  bvec(), bvec()],
        out_specs=[pl.BlockSpec((bm, d), lambda i, j: (i * nt + j, 0)),
                   pl.BlockSpec((bm, d), lambda i, j: (i * nt + j, 0)),
                   pl.BlockSpec((bm * ROW_SUBLANES, 128), lambda i, j: (i * nt + j, 0))],
        out_shape=[jax.ShapeDtypeStruct((t, d), F32), jax.ShapeDtypeStruct((t, d), BF16),
                   jax.ShapeDtypeStruct((t * ROW_SUBLANES, 128), I32)],
        compiler_params=_params(("arbitrary", "arbitrary"), 56),
    )(yl, yh, wl, wh, x, gt1, g2, sc2, sh2)


def _route_kernel(h_ref, wr_ref, bias_ref, idx_ref, w_ref, rank_ref, cnt_ref, carry_ref):
    step = pl.program_id(0)
    tm = h_ref.shape[0]
    ne = wr_ref.shape[0]
    gsz = ne // N_GROUPS
    neg = -jnp.inf

    @pl.when(step == 0)
    def _():
        carry_ref[...] = jnp.zeros_like(carry_ref)

    logits = _dot_nt(wr_ref[...], h_ref[...])
    scores = _sigmoid(logits)
    sel = scores + bias_ref[...]

    sel3 = sel.reshape(N_GROUPS, gsz, tm)
    pos3 = lax.broadcasted_iota(I32, (N_GROUPS, gsz, tm), 1)
    m1 = jnp.max(sel3, axis=1, keepdims=True)
    i1 = jnp.min(jnp.where(sel3 == m1, pos3, gsz), axis=1, keepdims=True)
    m2 = jnp.max(jnp.where(pos3 == i1, neg, sel3), axis=1, keepdims=True)
    gs = (m1 + m2).reshape(N_GROUPS, tm)

    gidx = lax.broadcasted_iota(I32, (N_GROUPS, tm), 0)
    beaten = jnp.zeros((N_GROUPS, tm), I32)
    for gp in range(N_GROUPS):
        other = gs[gp:gp + 1, :]
        beats = (other > gs) | ((other == gs) & (gp < gidx))
        beaten = beaten + beats.astype(I32)
    gkeep = (beaten < TOPK_GROUPS).reshape(N_GROUPS, 1, tm)
    cur = jnp.where(gkeep, sel3, neg).reshape(ne, tm)

    eidx = lax.broadcasted_iota(I32, (ne, tm), 0)
    picked = jnp.zeros((ne, tm), jnp.bool_)
    idx_rows, w_rows = [], []
    for _ in range(TOP_K):
        m = jnp.max(cur, axis=0, keepdims=True)
        ik = jnp.min(jnp.where(cur == m, eidx, ne), axis=0, keepdims=True)
        hit = eidx == ik
        w_rows.append(jnp.sum(jnp.where(hit, scores, 0.0), axis=0, keepdims=True))
        idx_rows.append(ik)
        cur = jnp.where(hit, neg, cur)
        picked = picked | hit
    w = jnp.concatenate(w_rows, axis=0)
    idx_ref[...] = jnp.concatenate(idx_rows, axis=0)
    w_ref[...] = w / jnp.sum(w, axis=0, keepdims=True) * ROUTED_SCALE

    pf = picked.astype(F32)
    ta = lax.broadcasted_iota(I32, (tm, tm), 0)
    tb = lax.broadcasted_iota(I32, (tm, tm), 1)
    before = (ta < tb).astype(BF16)
    cnt = _dot(pf.astype(BF16), before) + carry_ref[...]
    rank_rows = [jnp.sum(jnp.where(eidx == ik, cnt, 0.0), axis=0, keepdims=True) for ik in idx_rows]
    rank_ref[...] = jnp.concatenate(rank_rows, axis=0).astype(I32)
    total = carry_ref[...] + jnp.sum(pf, axis=1, keepdims=True)
    carry_ref[...] = total
    cnt_ref[...] = jnp.broadcast_to(total, cnt_ref.shape).astype(I32)


def _route(h2, wr_t, bias_col, tm):
    t, d = h2.shape
    ne = wr_t.shape[0]
    row = lambda: pl.BlockSpec((TOP_K, tm), lambda i: (0, i))
    return pl.pallas_call(
        _route_kernel,
        grid=(t // tm,),
        in_specs=[pl.BlockSpec((tm, d), lambda i: (i, 0)),
                  pl.BlockSpec((ne, d), lambda i: (0, 0)),
                  pl.BlockSpec((ne, 1), lambda i: (0, 0))],
        out_specs=[row(), row(), row(), pl.BlockSpec((ne, 128), lambda i: (0, 0))],
        out_shape=[jax.ShapeDtypeStruct((TOP_K, t), I32), jax.ShapeDtypeStruct((TOP_K, t), F32),
                   jax.ShapeDtypeStruct((TOP_K, t), I32), jax.ShapeDtypeStruct((ne, 128), I32)],
        scratch_shapes=[pltpu.VMEM((ne, 1), F32)],
        compiler_params=_params(("arbitrary",), 40),
    )(h2, wr_t, bias_col)


def _slots_kernel(ps_ref, idx_ref, rank_ref, o_ref):
    idx = idx_ref[...]

    def body(e, acc):
        return jnp.where(idx == e, ps_ref[e], acc)

    row = lax.fori_loop(0, ps_ref.shape[0], body, jnp.zeros(idx.shape, I32)) + rank_ref[...]
    o_ref[...] = row * ROW_SUBLANES


def _slots(pstarts, idx, rank, tm):
    k, t = idx.shape
    blk = lambda: pl.BlockSpec((k, tm), lambda i, ps: (0, i))
    return pl.pallas_call(
        _slots_kernel,
        grid_spec=pltpu.PrefetchScalarGridSpec(
            num_scalar_prefetch=1, grid=(t // tm,), in_specs=[blk(), blk()], out_specs=blk()),
        out_shape=jax.ShapeDtypeStruct((k, t), I32),
        compiler_params=_params(("arbitrary",), 32),
    )(pstarts, idx, rank)


def _pad_bits():
    bit = EXPERT_ROWS // 2
    while bit >= 1:
        yield bit
        bit //= 2


def _row_tile(ref, first_sublane_row, rows=1):
    return ref.at[pl.ds(pl.multiple_of(first_sublane_row, ROW_SUBLANES), rows * ROW_SUBLANES), :]


def _dispatch_kernel(cnt_ref, ps_ref, pc_ref, dest_ref, h_ref, xs_ref, zero_ref, sem, pad_sem):
    tm = h_ref.shape[0] // ROW_SUBLANES

    def copy(t, k):
        return pltpu.make_async_copy(_row_tile(h_ref, t * ROW_SUBLANES),
                                     _row_tile(xs_ref, dest_ref[t * TOP_K + k]), sem)

    def start(t, carry):
        for k in range(TOP_K):
            copy(t, k).start(priority=k % 2)
        return carry

    def wait(t, carry):
        for k in range(TOP_K):
            copy(t, k).wait()
        return carry

    lax.fori_loop(0, tm, start, 0)

    @pl.when(pl.program_id(0) == 0)
    def _():
        zero_ref[...] = jnp.zeros_like(zero_ref)

        def pad_copy(first, rows):
            return pltpu.make_async_copy(_row_tile(zero_ref, 0, rows),
                                         _row_tile(xs_ref, first * ROW_SUBLANES, rows), pad_sem)

        def pads(e, fn):
            pad = pc_ref[e] - cnt_ref[e]
            lo = ps_ref[e] + cnt_ref[e]
            for bit in _pad_bits():
                @pl.when((pad & bit) != 0)
                def _():
                    fn(pad_copy(lo + (pad // (2 * bit)) * (2 * bit), bit))

        def start_pads(e, carry):
            pads(e, lambda cp: cp.start())
            return carry

        def wait_pads(e, carry):
            pads(e, lambda cp: cp.wait())
            return carry

        lax.fori_loop(0, cnt_ref.shape[0], start_pads, 0)
        lax.fori_loop(0, cnt_ref.shape[0], wait_pads, 0)

    lax.fori_loop(0, tm, wait, 0)


def _dispatch(counts, pstarts, pcounts, dest, h2p, n_rows, tm):
    t = h2p.shape[0] // ROW_SUBLANES
    return pl.pallas_call(
        _dispatch_kernel,
        grid_spec=pltpu.PrefetchScalarGridSpec(
            num_scalar_prefetch=3,
            grid=(t // tm,),
            in_specs=[pl.BlockSpec((tm * TOP_K,), lambda i, *_: (i,), memory_space=pltpu.SMEM),
                      pl.BlockSpec((tm * ROW_SUBLANES, 128), lambda i, *_: (i, 0))],
            out_specs=pl.BlockSpec(memory_space=pl.ANY),
            scratch_shapes=[pltpu.VMEM((EXPERT_ROWS // 2 * ROW_SUBLANES, 128), I32),
                            pltpu.SemaphoreType.DMA, pltpu.SemaphoreType.DMA],
        ),
        out_shape=jax.ShapeDtypeStruct((n_rows * ROW_SUBLANES, 128), I32),
        compiler_params=_params(("arbitrary",), 32),
    )(counts, pstarts, pcounts, dest, h2p)


def _expert_kernel(be_ref, rows_ref, nu_ref, first_ref, slot_ref, ahead_ref, warm_ref, x_ref, wg_hbm, wu_hbm, wd_hbm,
                   o_ref, wgf, wuf, wdf, wgb, wub, wdb, sem):
    nb = pl.program_id(0)
    half_f = wd_hbm.shape[1] // 2

    def fetch(e, s):
        lo, hi = pl.ds(0, half_f), pl.ds(half_f, half_f)
        return (pltpu.make_async_copy(wg_hbm.at[e], wgf.at[s], sem.at[s, 0]),
                pltpu.make_async_copy(wu_hbm.at[e], wuf.at[s], sem.at[s, 1]),
                pltpu.make_async_copy(wd_hbm.at[e, lo], wdf.at[s, lo], sem.at[s, 2]),
                pltpu.make_async_copy(wd_hbm.at[e, hi], wdf.at[s, hi], sem.at[s, 3]))

    def start(e, s):
        for i, cp in enumerate(fetch(e, s)):
            cp.start(priority=i % 2)

    @pl.when(nb == 0)
    def _():
        for i in range(WEIGHT_SLOTS - 1):
            @pl.when(warm_ref[i] >= 0)
            def _():
                start(warm_ref[i], i)

    @pl.when(first_ref[nb] == 1)
    def _():
        s = slot_ref[nb]
        for cp in fetch(be_ref[nb], s):
            cp.wait()

        @pl.when(ahead_ref[nb] >= 0)
        def _():
            start(ahead_ref[nb], (s + WEIGHT_SLOTS - 1) % WEIGHT_SLOTS)

        wgb[...] = wgf[s].astype(BF16)
        wub[...] = wuf[s].astype(BF16)
        wdb[...] = wdf[s].astype(BF16)

    def mlp(rows):
        x = _unpack_halves(_load_row_tiles(x_ref, rows)).astype(BF16)
        g = _dot(x, wgb[...])
        u = _dot(x, wub[...])
        a = (g * _sigmoid(g)) * u
        _store_row_tiles(o_ref, _pack_halves(_dot(a.astype(BF16), wdb[...])))

    half = EXPERT_ROWS // 2

    @pl.when(rows_ref[nb] > half)
    def _():
        mlp(EXPERT_ROWS)

    @pl.when((rows_ref[nb] > 0) & (rows_ref[nb] <= half))
    def _():
        mlp(half)
        o_ref[half * ROW_SUBLANES:, :] = jnp.zeros((half * ROW_SUBLANES, 128), I32)


def _experts(block_e, rows, n_used, first, slot, ahead_e, warm_e, xs, w_gate, w_up, w_down):
    r, dp = xs.shape
    ne, d, f = w_gate.shape
    bm = EXPERT_ROWS * ROW_SUBLANES
    ws = WEIGHT_SLOTS
    row_blk = lambda nb, be, rw, nu, *_: (jnp.minimum(nb, nu[0] - 1), 0)
    return pl.pallas_call(
        _expert_kernel,
        grid_spec=pltpu.PrefetchScalarGridSpec(
            num_scalar_prefetch=7,
            grid=(r // bm,),
            in_specs=[pl.BlockSpec((bm, dp), row_blk),
                      pl.BlockSpec(memory_space=pl.ANY),
                      pl.BlockSpec(memory_space=pl.ANY),
                      pl.BlockSpec(memory_space=pl.ANY)],
            out_specs=pl.BlockSpec((bm, dp), row_blk),
            scratch_shapes=[pltpu.VMEM((ws, d, f), F32), pltpu.VMEM((ws, d, f), F32), pltpu.VMEM((ws, f, d), F32),
                            pltpu.VMEM((d, f), BF16), pltpu.VMEM((d, f), BF16), pltpu.VMEM((f, d), BF16),
                            pltpu.SemaphoreType.DMA((ws, 4))],
        ),
        out_shape=jax.ShapeDtypeStruct((r, dp), I32),
        compiler_params=_params(("arbitrary",), 60),
    )(block_e, rows, n_used, first, slot, ahead_e, warm_e, xs, w_gate, w_up, w_down)


def _combine_kernel(dest_ref, dnext_ref, w_ref, x1_ref, h_ref, gt_ref, fg_ref, wg_ref, wu_ref, wd_ref, ys_ref,
                    o_ref, buf0, buf1, routed_ref, sem):
    buf = (buf0, buf1)
    i = pl.program_id(0)
    last = pl.num_programs(0) - 1
    tm = x1_ref.shape[0]

    def copy(idx_ref, s, t, k):
        return pltpu.make_async_copy(_row_tile(ys_ref, idx_ref[t * TOP_K + k]),
                                     _row_tile(buf[s].at[k], t * ROW_SUBLANES), sem.at[s])

    def start_token(idx_ref, s, t):
        for k in range(TOP_K):
            copy(idx_ref, s, t, k).start(priority=k % 2)

    def wait_tile(s):
        def body(t, carry):
            for k in range(TOP_K):
                copy(dest_ref, s, t, k).wait()
            return carry
        lax.fori_loop(0, tm, body, 0)

    @pl.when(i == 0)
    def _():
        def body(t, carry):
            start_token(dest_ref, 0, t)
            return carry
        lax.fori_loop(0, tm, body, 0)

    def gather_reduce(slot):
        wait_tile(slot)

        def group(g, carry):
            for r in range(8):
                start_token(dnext_ref, 1 - slot, g * 8 + r)
            base = pl.multiple_of(g * (8 * ROW_SUBLANES), 8 * ROW_SUBLANES)
            w8 = w_ref[pl.ds(pl.multiple_of(g * 8, 8), 8), :]
            lo = [None] * ROW_SUBLANES
            hi = [None] * ROW_SUBLANES
            for k in range(TOP_K):
                wk = w8[:, k:k + 1]
                for s in range(ROW_SUBLANES):
                    p = buf[slot][k, pl.ds(base + s, 8, stride=ROW_SUBLANES), :]
                    l = lax.bitcast_convert_type(lax.shift_left(p, 16), F32) * wk
                    h = lax.bitcast_convert_type(p & -65536, F32) * wk
                    lo[s] = l if k == 0 else lo[s] + l
                    hi[s] = h if k == 0 else hi[s] + h
            routed_ref[pl.ds(pl.multiple_of(g * 8, 8), 8), :] = jnp.concatenate(lo + hi, axis=1)
            return carry

        lax.fori_loop(0, tm // 8, group, 0)

    for slot in range(2):
        pl.when(i % 2 == slot)(functools.partial(gather_reduce, slot))

    x = h_ref[...]
    g = _dot(x, wg_ref[...])
    u = _dot(x, wu_ref[...])
    acc = _dot(((g * _sigmoid(g)) * u).astype(BF16), wd_ref[...])
    y = x1_ref[...] + gt_ref[0] * (routed_ref[...] + acc)
    ms = jnp.mean(y * y, axis=-1, keepdims=True)
    o_ref[...] = y * lax.rsqrt(ms + RMS_EPS) * fg_ref[...]

    for slot in range(2):
        pl.when((i == last) & (i % 2 == slot))(functools.partial(wait_tile, 1 - slot))


def _combine(dest, w_tok, x1, h2, gt2, fg, wg, wu, wd, ys, seq, tm):
    t, d = x1.shape
    f = wg.shape[1]
    nt = t // tm
    per_b = seq // tm
    const = lambda shape: pl.BlockSpec(shape, lambda i: (0, 0))
    return pl.pallas_call(
        _combine_kernel,
        grid=(nt,),
        in_specs=[pl.BlockSpec((tm * TOP_K,), lambda i: (i,), memory_space=pltpu.SMEM),
                  pl.BlockSpec((tm * TOP_K,), lambda i: (jnp.minimum(i + 1, nt - 1),), memory_space=pltpu.SMEM),
                  pl.BlockSpec((tm, TOP_K), lambda i: (i, 0)),
                  pl.BlockSpec((tm, d), lambda i: (i, 0)),
                  pl.BlockSpec((tm, d), lambda i: (i, 0)),
                  pl.BlockSpec((1, 1, d), lambda i: (i // per_b, 0, 0)),
                  const((1, d)), const((d, f)), const((d, f)), const((f, d)),
                  pl.BlockSpec(memory_space=pl.ANY)],
        out_specs=pl.BlockSpec((tm, d), lambda i: (i, 0)),
        out_shape=jax.ShapeDtypeStruct((t, d), F32),
        scratch_shapes=[pltpu.VMEM((TOP_K, tm * ROW_SUBLANES, 128), I32),
                        pltpu.VMEM((TOP_K, tm * ROW_SUBLANES, 128), I32), pltpu.VMEM((tm, d), F32),
                        pltpu.SemaphoreType.DMA((2,))],
        compiler_params=_params(("arbitrary",), 56),
    )(dest, dest, w_tok, x1, h2, gt2, fg, wg, wu, wd, ys)


def _block_tables(pends, row_ends, n_blocks):
    bm = EXPERT_ROWS
    n_used = (pends[-1] // bm).astype(I32)
    j = jnp.arange(n_blocks, dtype=I32)
    jc = jnp.minimum(j, n_used - 1)
    block_e = jnp.sum((pends[None, :] <= (jc * bm)[:, None]).astype(I32), axis=1)
    rows = jnp.where(j < n_used, jnp.clip(row_ends[block_e] - j * bm, 0, bm), 0).astype(I32)
    prev_e = jnp.concatenate([jnp.full((1,), -1, I32), block_e[:-1]])
    first = ((block_e != prev_e) & (j < n_used)).astype(I32)
    slot = (jnp.cumsum(first) - 1) % WEIGHT_SLOTS
    big = jnp.int32(n_blocks)
    later = lax.cummin(jnp.where(first == 1, j, big)[::-1])[::-1]
    nxt = jnp.concatenate([later[1:], big[None]])
    hop = lambda at: jnp.where(at < big, nxt[jnp.minimum(at, n_blocks - 1)], big)
    expert_at = lambda at: jnp.where(at < big, block_e[jnp.minimum(at, n_blocks - 1)], -1).astype(I32)
    ahead = j
    warm = []
    for _ in range(WEIGHT_SLOTS - 1):
        warm.append(expert_at(ahead[0]))
        ahead = hop(ahead)
    return block_e, rows, n_used[None], first, slot.astype(I32), expert_at(ahead), jnp.stack(warm)


def kernel(x, c, w_ada, b_ada, norm1_g, w_in, conv_w, conv_b, lru_wa, lru_ba, lru_wx, lru_bx,
           lru_lambda, hgrn_lb, hgrn_norm_g, w_out, norm2_g, w_router, router_bias,
           w_gate, w_up, w_down, ws_gate, ws_up, ws_down, final_g):
    batch, seq, d = x.shape
    t = batch * seq
    depth = w_ada.shape[0]
    d_lru = conv_w.shape[2]
    d_hgrn = d - d_lru
    heads = d_hgrn // HEAD_DIM
    ne = w_router.shape[2]
    assert d_lru % LRU_BLOCK == 0 and d_hgrn % HEAD_DIM == 0 and seq % 512 == 0 and ne % (8 * N_GROUPS) == 0
    assert depth == 1, "the final norm is fused into the combine kernel"
    assert d == 2 * 128 * ROW_SUBLANES, "a packed row must be exactly one (ROW_SUBLANES, 128) tile"
    l = 0

    lower_bounds = jnp.cumsum(jax.nn.softmax(hgrn_lb.astype(F32), axis=0), axis=0)
    c_pad = jnp.pad(c, ((0, -batch % 8), (0, 0)))
    mod = _adaln_mod(c_pad, w_ada[l], b_ada[l][None, :])[:batch]
    sh1, sc1, gt1, sh2, sc2, gt2 = [m[:, None, :] for m in jnp.split(mod, 6, axis=-1)]

    h = _norm_mod(x, norm1_g[l][None, :], sc1, sh1, 512)
    proj = _matmul(h, w_in[l].astype(BF16), min(1024, t), 1024, BF16)
    y_lru = _rg_lru(proj, conv_w[l], conv_b[l][None, :], lru_wa[l].astype(BF16), lru_ba[l][None, :],
                    lru_wx[l].astype(BF16), lru_bx[l][None, :], lru_lambda[l][None, :], batch, seq, 256)
    y_hgrn = _hgrn2(proj, lower_bounds[l][None, :], hgrn_norm_g[l][None, :], batch, seq, heads,
                    2 * d_lru // HEAD_DIM, min(seq, 2048), 4)
    wo = w_out[l].astype(BF16)
    x1, h2, h2p = _out_proj(y_lru, y_hgrn, wo[:d_lru], wo[d_lru:], x, gt1, norm2_g[l][None, :], sc2, sh2, 512)

    idx, w_sel, rank, counts = _route(h2, w_router[l].T.astype(BF16), router_bias[l][:, None], 512)
    counts = counts[:, 0]
    bm = EXPERT_ROWS
    pcounts = (counts + bm - 1) // bm * bm
    pends = jnp.cumsum(pcounts)
    pstarts = pends - pcounts
    n_rows = -(-(t * TOP_K + ne * (bm - 1)) // bm) * bm
    dest = _slots(pstarts, idx, rank, min(2048, t)).T.reshape(-1)
    xs = _dispatch(counts, pstarts, pcounts, dest, h2p, n_rows, 256)
    ys = _experts(*_block_tables(pends, pstarts + counts, n_rows // bm), xs, w_gate[l], w_up[l], w_down[l])
    out = _combine(dest, w_sel.T, x1, h2, gt2, final_g[None, :], ws_gate[l].astype(BF16),
                   ws_up[l].astype(BF16), ws_down[l].astype(BF16), ys, seq, 256)
    return out.reshape(batch, seq, d)
```

```python
import functools

import jax
import jax.numpy as jnp
from jax import lax
from jax.experimental import pallas as pl
from jax.experimental.pallas import tpu as pltpu

F32 = jnp.float32
BF16 = jnp.bfloat16
I32 = jnp.int32

RMS_EPS = 1e-6
LRU_C = 8.0
LRU_BLOCK = 128
CONV_WIDTH = 4
HEAD_DIM = 128
CHUNK = 64
SUB = 16
N_GROUPS = 8
TOPK_GROUPS = 4
TOP_K = 8
ROUTED_SCALE = 2.5
EXPERT_ROWS = 256
ROW_SUBLANES = 8
WEIGHT_SLOTS = 3
MIB = 1024 * 1024


def _params(sem, vmem_mib):
    return pltpu.CompilerParams(dimension_semantics=sem, vmem_limit_bytes=vmem_mib * MIB)


def _sigmoid(x):
    return 0.5 * jnp.tanh(0.5 * x) + 0.5


def _dot(a, b):
    return jnp.dot(a, b, preferred_element_type=F32)


def _dot_nt(a, b):
    return lax.dot_general(a, b, (((1,), (1,)), ((), ())), preferred_element_type=F32)


def _dot_tn(a, b):
    return lax.dot_general(a, b, (((0,), (0,)), ((), ())), preferred_element_type=F32)


def _mod_kernel(c_ref, w_ref, b_ref, o_ref):
    c = c_ref[...]
    cond = c * _sigmoid(c)
    o_ref[...] = _dot(cond.astype(BF16), w_ref[...].astype(BF16)) + b_ref[...]


def _adaln_mod(c_pad, w_ada, b_ada):
    rows, d = c_pad.shape
    n = w_ada.shape[1]
    bn = 1024
    return pl.pallas_call(
        _mod_kernel,
        grid=(n // bn,),
        in_specs=[pl.BlockSpec((rows, d), lambda j: (0, 0)),
                  pl.BlockSpec((d, bn), lambda j: (0, j)),
                  pl.BlockSpec((1, bn), lambda j: (0, j))],
        out_specs=pl.BlockSpec((rows, bn), lambda j: (0, j)),
        out_shape=jax.ShapeDtypeStruct((rows, n), F32),
        compiler_params=_params(("arbitrary",), 40),
    )(c_pad, w_ada, b_ada)


def _norm_mm_kernel(x_ref, g_ref, sc_ref, sh_ref, w_ref, o_ref):
    x = x_ref[0]
    ms = jnp.mean(x * x, axis=-1, keepdims=True)
    y = x * lax.rsqrt(ms + RMS_EPS) * g_ref[...]
    h = (y * (1.0 + sc_ref[0]) + sh_ref[0]).astype(BF16)
    o_ref[...] = _dot(h, w_ref[...]).astype(o_ref.dtype)


def _norm_matmul(x, g, sc, sh, w, bm, bn, out_dtype):
    b, l, k = x.shape
    n = w.shape[1]
    nt = l // bm
    bvec = lambda: pl.BlockSpec((1, 1, k), lambda j, i, t: (i, 0, 0))
    return pl.pallas_call(
        _norm_mm_kernel,
        grid=(n // bn, b, nt),
        in_specs=[pl.BlockSpec((1, bm, k), lambda j, i, t: (i, t, 0)),
                  pl.BlockSpec((1, k), lambda j, i, t: (0, 0)),
                  bvec(), bvec(),
                  pl.BlockSpec((k, bn), lambda j, i, t: (0, j))],
        out_specs=pl.BlockSpec((bm, bn), lambda j, i, t: (i * nt + t, j)),
        out_shape=jax.ShapeDtypeStruct((b * l, n), out_dtype),
        compiler_params=_params(("arbitrary", "arbitrary", "arbitrary"), 56),
    )(x, g, sc, sh, w)


def _shift_rows(u, prev, k):
    if k == 0:
        return u
    tt, c = u.shape
    rot = pltpu.roll(u.reshape(tt // 8, 8, c), k, 1)
    before = jnp.concatenate([pltpu.roll(prev, k, 0)[None], rot[:-1]], axis=0)
    r = lax.broadcasted_iota(I32, rot.shape, 1)
    return jnp.where(r < k, before, rot).reshape(tt, c)


def _lru_kernel(u_ref, z_ref, cw_ref, cb_ref, wa_ref, ba_ref, wx_ref, bx_ref, lam_ref,
                o_ref, prev_ref, h_ref):
    j = pl.program_id(1)
    tt, c = u_ref.shape

    @pl.when(j == 0)
    def _():
        prev_ref[...] = jnp.zeros_like(prev_ref)
        h_ref[...] = jnp.zeros_like(h_ref)

    u = u_ref[...].astype(F32)
    prev = prev_ref[...]
    xc = jnp.broadcast_to(cb_ref[...], (tt, c))
    for w in range(CONV_WIDTH):
        xc = xc + cw_ref[w:w + 1, :] * _shift_rows(u, prev, CONV_WIDTH - 1 - w)
    prev_ref[...] = u[tt - 8:, :]

    xcb = xc.astype(BF16)
    ra, rx = [], []
    for blk in range(c // LRU_BLOCK):
        xs = xcb[:, blk * LRU_BLOCK:(blk + 1) * LRU_BLOCK]
        ra.append(_dot(xs, wa_ref[blk]))
        rx.append(_dot(xs, wx_ref[blk]))
    r = _sigmoid(jnp.concatenate(ra, axis=1) + ba_ref[...])
    ig = _sigmoid(jnp.concatenate(rx, axis=1) + bx_ref[...])

    nl = -lam_ref[...]
    softplus = jnp.maximum(nl, 0.0) + jnp.log1p(jnp.exp(-jnp.abs(nl)))
    log_a = (-LRU_C) * r * softplus
    a = jnp.exp(log_a)
    mult = jnp.sqrt(1.0 - jnp.exp(2.0 * log_a))
    row = lax.broadcasted_iota(I32, (tt, c), 0)
    mult = jnp.where((row == 0) & (j == 0), 1.0, mult)
    bv = mult * ig * xc

    a3 = a.reshape(tt // 8, 8, c)
    b3 = bv.reshape(tt // 8, 8, c)
    in_group = lax.broadcasted_iota(I32, a3.shape, 1)
    s = 1
    while s < 8:
        keep = in_group >= s
        b3 = jnp.where(keep, a3 * pltpu.roll(b3, s, 1) + b3, b3)
        a3 = jnp.where(keep, a3 * pltpu.roll(a3, s, 1), a3)
        s *= 2
    carry = h_ref[...]
    groups = []
    for gi in range(tt // 8):
        hg = b3[gi] + a3[gi] * carry
        carry = hg[7:8, :]
        groups.append(hg)
    h = jnp.concatenate(groups, axis=0)
    h_ref[...] = carry

    z = z_ref[...].astype(F32)
    gelu = 0.5 * z * (1.0 + jnp.tanh(0.7978845608028654 * (z + 0.044715 * (z * z * z))))
    o_ref[...] = (h * gelu).astype(o_ref.dtype)


def _rg_lru(proj, conv_w, conv_b, wa, ba, wx, bx, lam, batch, seq, tt):
    t = proj.shape[0]
    c = conv_w.shape[1]
    nt = seq // tt
    vec = lambda: pl.BlockSpec((1, c), lambda i, j: (0, 0))
    mat = lambda: pl.BlockSpec(wa.shape, lambda i, j: (0, 0, 0))
    return pl.pallas_call(
        _lru_kernel,
        grid=(batch, nt),
        in_specs=[pl.BlockSpec((tt, c), lambda i, j: (i * nt + j, 0)),
                  pl.BlockSpec((tt, c), lambda i, j: (i * nt + j, 1)),
                  pl.BlockSpec((CONV_WIDTH, c), lambda i, j: (0, 0)),
                  vec(), mat(), vec(), mat(), vec(), vec()],
        out_specs=pl.BlockSpec((tt, c), lambda i, j: (i * nt + j, 0)),
        out_shape=jax.ShapeDtypeStruct((t, c), BF16),
        scratch_shapes=[pltpu.VMEM((8, c), F32), pltpu.VMEM((1, c), F32)],
        compiler_params=_params(("arbitrary", "arbitrary"), 48),
    )(proj, proj, conv_w, conv_b, wa, ba, wx, bx, lam)


def _hgrn_lanes(wid):
    return [slice(h * HEAD_DIM, (h + 1) * HEAD_DIM) for h in range(wid // HEAD_DIM)]


def _per_head(wid, fn):
    return jnp.concatenate([fn(h, ln) for h, ln in enumerate(_hgrn_lanes(wid))], axis=1)


_N_SUB = CHUNK // SUB
_STACK_ROWS = 8 * 16 + 8 * 8
_KT_OFFSET = [SUB * i * (i - 1) // 2 for i in range(_N_SUB + 1)]


def _hgrn_front_shapes(wid):
    return [pltpu.VMEM((CHUNK, wid), BF16),
            pltpu.VMEM((CHUNK, wid), BF16),
            pltpu.VMEM((CHUNK, wid), BF16),
            pltpu.VMEM((CHUNK, wid), BF16),
            pltpu.VMEM((1, wid), F32),
            pltpu.VMEM((CHUNK, wid), F32),
            pltpu.VMEM((CHUNK, wid), F32),
            pltpu.VMEM((_KT_OFFSET[_N_SUB], wid), BF16),
            pltpu.VMEM((_N_SUB * _STACK_ROWS, wid), BF16)]


def _hgrn_front(qi, fi, v, gi, lb, out):
    qe_ref, qt_ref, vb_ref, kd_ref, decay_ref, gate_ref, v_ref, kt_ref, stack_ref = out
    n, wid = qi.shape
    q = qi * _sigmoid(qi)
    f = lb + (1.0 - lb) * _sigmoid(fi)
    row = lax.broadcasted_iota(I32, (n, wid), 0)
    b = jnp.log2(f)
    s = 1
    while s < n:
        b = b + jnp.where(row >= s, pltpu.roll(b, s, 0), 0.0)
        s *= 2
    c = b - jnp.log2(1.0 - f)
    qe_ref[...] = (q * jnp.exp2(b)).astype(BF16)
    refb = jnp.concatenate(
        [jnp.broadcast_to(b[i * SUB:i * SUB + 1, :], (SUB, wid)) for i in range(_N_SUB)], axis=0)
    qt_ref[...] = (q * jnp.exp2(b - refb)).astype(BF16)
    vb_ref[...] = v.astype(BF16)
    v_ref[...] = v
    gate_ref[...] = gi * _sigmoid(gi)
    b_last = b[n - 1:n, :]
    kd_ref[...] = jnp.exp2(b_last - c).astype(BF16)
    decay_ref[...] = jnp.exp2(b_last)
    row8 = lax.broadcasted_iota(I32, (8, wid), 0)
    for i in range(_N_SUB):
        lo = i * SUB
        if i > 0:
            kt_ref[_KT_OFFSET[i]:_KT_OFFSET[i + 1], :] = jnp.exp2(b[lo:lo + 1, :] - c[:lo, :]).astype(BF16)
        q0, q1 = q[lo:lo + 8, :], q[lo + 8:lo + SUB, :]
        b0, b1 = b[lo:lo + 8, :], b[lo + 8:lo + SUB, :]
        base = i * _STACK_ROWS
        pending = None
        for s_ in range(SUB):
            cs = c[lo + s_:lo + s_ + 1, :]
            p1 = q1 * jnp.exp2(b1 - cs)
            if s_ < 8:
                pair = [jnp.where(row8 >= s_, q0 * jnp.exp2(b0 - cs), 0.0), p1]
                at = base + 16 * s_
            elif pending is None:
                pending = jnp.where(row8 >= s_ - 8, p1, 0.0)
                continue
            else:
                pair = [pending, jnp.where(row8 >= s_ - 8, p1, 0.0)]
                pending = None
                at = base + 64 + 8 * (s_ - 1)
            stack_ref[at:at + 16, :] = jnp.concatenate(pair, axis=0).astype(BF16)


def _hgrn_products(front, states, ones_b):
    qe_ref, qt_ref, _, _, _, _, _, kt_ref, stack_ref = front
    wid = qe_ref.shape[1]
    lanes = _hgrn_lanes(wid)
    o_state = _per_head(wid, lambda h, ln: _dot_nt(qe_ref[:, ln], states[h].astype(BF16)))
    scores = [[_dot_nt(qt_ref[i * SUB:(i + 1) * SUB, ln], kt_ref[_KT_OFFSET[i]:_KT_OFFSET[i + 1], ln]).astype(BF16)
               for ln in lanes] for i in range(1, _N_SUB)]
    reds = [_per_head(wid, lambda h, ln: _dot(stack_ref[i * _STACK_ROWS:(i + 1) * _STACK_ROWS, ln], ones_b))
            for i in range(_N_SUB)]
    return o_state, scores, reds


def _hgrn_finish(front, products, states, ng):
    _, _, vb_ref, kd_ref, decay_ref, gate_ref, v_ref, _, _ = front
    o_state, scores, reds = products
    n, wid = vb_ref.shape
    outs = []
    for i in range(_N_SUB):
        lo = i * SUB
        acc = o_state[lo:lo + SUB, :]
        if i > 0:
            sc = scores[i - 1]
            acc = acc + _per_head(wid, lambda h, ln: _dot(sc[h], vb_ref[:lo, ln]))
        red = reds[i]
        acc0 = jnp.zeros((8, wid), F32)
        acc1 = jnp.zeros((8, wid), F32)
        for s_ in range(SUB):
            vs = v_ref[lo + s_:lo + s_ + 1, :]
            if s_ < 8:
                acc0 = acc0 + red[16 * s_:16 * s_ + 8, :] * vs
                acc1 = acc1 + red[16 * s_ + 8:16 * s_ + 16, :] * vs
            else:
                acc1 = acc1 + red[64 + 8 * s_:72 + 8 * s_, :] * vs
        outs.append(acc + jnp.concatenate([acc0, acc1], axis=0))
    o = jnp.concatenate(outs, axis=0)
    decay = decay_ref[...]
    new_states = [states[h] * decay[:, ln] + _dot_tn(vb_ref[:, ln], kd_ref[:, ln])
                  for h, ln in enumerate(_hgrn_lanes(wid))]
    oo = o * o
    scale = _per_head(wid, lambda h, ln: jnp.broadcast_to(
        lax.rsqrt(jnp.mean(oo[:, ln], axis=-1, keepdims=True) + RMS_EPS), (n, HEAD_DIM)) * ng)
    return o * scale * gate_ref[...], new_states


def _hgrn_kernel(q_ref, f_ref, v_ref, g_ref, lb_ref, ng_ref, o_ref, st_ref, *front_refs):
    @pl.when(pl.program_id(2) == 0)
    def _():
        st_ref[...] = jnp.zeros_like(st_ref)

    lb = lb_ref[...]
    ng = ng_ref[...]
    heads = q_ref.shape[1] // HEAD_DIM
    n_chunks = q_ref.shape[0] // CHUNK
    ones_b = jnp.ones((HEAD_DIM, HEAD_DIM), BF16)
    half = len(front_refs) // 2
    fronts = (front_refs[:half], front_refs[half:])

    def chunk_rows(ci):
        return pl.ds(pl.multiple_of(ci * CHUNK, CHUNK), CHUNK)

    def front(ci, out):
        rows = chunk_rows(ci)
        _hgrn_front(q_ref[rows, :].astype(F32), f_ref[rows, :].astype(F32),
                    v_ref[rows, :].astype(F32), g_ref[rows, :].astype(F32), lb, out)

    def back(ci, cur, nxt):
        states = [st_ref[h] for h in range(heads)]
        products = _hgrn_products(cur, states, ones_b)
        front(jnp.minimum(ci + 1, n_chunks - 1), nxt)
        y, new_states = _hgrn_finish(cur, products, states, ng)
        for h in range(heads):
            st_ref[h] = new_states[h]
        o_ref[chunk_rows(ci), :] = y.astype(o_ref.dtype)

    def body(pair, carry):
        back(2 * pair, fronts[0], fronts[1])
        back(2 * pair + 1, fronts[1], fronts[0])
        return carry

    front(0, fronts[0])
    lax.fori_loop(0, n_chunks // 2, body, 0)


def _hgrn2(proj, lb, norm_g, batch, seq, heads, col0, tt, hp):
    t = proj.shape[0]
    nt = seq // tt
    wid = hp * HEAD_DIM
    assert heads % hp == 0 and col0 % hp == 0
    col = lambda off: pl.BlockSpec((tt, wid), lambda i, h, j: (i * nt + j, (col0 + off * heads) // hp + h))
    return pl.pallas_call(
        _hgrn_kernel,
        grid=(batch, heads // hp, nt),
        in_specs=[col(0), col(1), col(2), col(3),
                  pl.BlockSpec((1, wid), lambda i, h, j: (0, h)),
                  pl.BlockSpec((1, HEAD_DIM), lambda i, h, j: (0, 0))],
        out_specs=pl.BlockSpec((tt, wid), lambda i, h, j: (i * nt + j, h)),
        out_shape=jax.ShapeDtypeStruct((t, heads * HEAD_DIM), BF16),
        scratch_shapes=[pltpu.VMEM((hp, HEAD_DIM, HEAD_DIM), F32)] + 2 * _hgrn_front_shapes(wid),
        compiler_params=_params(("arbitrary", "arbitrary", "arbitrary"), 48),
    )(proj, proj, proj, proj, lb, norm_g)


def _pack_halves(y):
    n = y.shape[1] // 2
    lo = lax.bitcast_convert_type(y[:, :n].astype(BF16).astype(F32), I32)
    hi = lax.bitcast_convert_type(y[:, n:].astype(BF16).astype(F32), I32)
    return (hi & -65536) | lax.shift_right_logical(lo, 16)


def _unpack_halves(p):
    lo = lax.bitcast_convert_type(lax.shift_left(p, 16), F32)
    hi = lax.bitcast_convert_type(p & -65536, F32)
    return jnp.concatenate([lo, hi], axis=1)


def _store_row_tiles(ref, packed):
    m = packed.shape[0]
    for s in range(ROW_SUBLANES):
        ref[pl.ds(s, m, stride=ROW_SUBLANES), :] = packed[:, s * 128:(s + 1) * 128]


def _load_row_tiles(ref, m):
    return jnp.concatenate([ref[pl.ds(s, m, stride=ROW_SUBLANES), :] for s in range(ROW_SUBLANES)], axis=1)


def _out_kernel(yl_ref, yh_ref, wl_ref, wh_ref, x_ref, gt_ref, g2_ref, sc_ref, sh_ref, x1_ref, h2_ref, h2p_ref):
    mix = _dot(yl_ref[...], wl_ref[...]) + _dot(yh_ref[...], wh_ref[...])
    x1 = x_ref[0] + gt_ref[0] * mix
    x1_ref[...] = x1
    ms = jnp.mean(x1 * x1, axis=-1, keepdims=True)
    y = x1 * lax.rsqrt(ms + RMS_EPS) * g2_ref[...]
    h2 = y * (1.0 + sc_ref[0]) + sh_ref[0]
    h2_ref[...] = h2.astype(BF16)
    _store_row_tiles(h2p_ref, _pack_halves(h2))


def _out_proj(yl, yh, wl, wh, x, gt1, g2, sc2, sh2, bm):
    b, l, d = x.shape
    t = b * l
    nt = l // bm
    cl, ch = yl.shape[1], yh.shape[1]
    bvec = lambda: pl.BlockSpec((1, 1, d), lambda i, j: (i, 0, 0))
    return pl.pallas_call(
        _out_kernel,
        grid=(b, nt),
        in_specs=[pl.BlockSpec((bm, cl), lambda i, j: (i * nt + j, 0)),
                  pl.BlockSpec((bm, ch), lambda i, j: (i * nt + j, 0)),
                  pl.BlockSpec((cl, d), lambda i, j: (0, 0)),
                  pl.BlockSpec((ch, d), lambda i, j: (0, 0)),
                  pl.BlockSpec((1, bm, d), lambda i, j: (i, j, 0)),
                  bvec(),
                  pl.BlockSpec((1, d), lambda i, j: (0, 0)),
                  bvec(), bvec()],
        out_specs=[pl.BlockSpec((bm, d), lambda i, j: (i * nt + j, 0)),
                   pl.BlockSpec((bm, d), lambda i, j: (i * nt + j, 0)),
                   pl.BlockSpec((bm * ROW_SUBLANES, 128), lambda i, j: (i * nt + j, 0))],
        out_shape=[jax.ShapeDtypeStruct((t, d), F32), jax.ShapeDtypeStruct((t, d), BF16),
                   jax.ShapeDtypeStruct((t * ROW_SUBLANES, 128), I32)],
        compiler_params=_params(("arbitrary", "arbitrary"), 56),
    )(yl, yh, wl, wh, x, gt1, g2, sc2, sh2)


def _route_kernel(h_ref, wr_ref, bias_ref, idx_ref, w_ref, rank_ref, cnt_ref, carry_ref):
    step = pl.program_id(0)
    tm = h_ref.shape[0]
    ne = wr_ref.shape[0]
    gsz = ne // N_GROUPS
    neg = -jnp.inf

    @pl.when(step == 0)
    def _():
        carry_ref[...] = jnp.zeros_like(carry_ref)

    logits = _dot_nt(wr_ref[...], h_ref[...])
    scores = _sigmoid(logits)
    sel = scores + bias_ref[...]

    sel3 = sel.reshape(N_GROUPS, gsz, tm)
    pos3 = lax.broadcasted_iota(I32, (N_GROUPS, gsz, tm), 1)
    m1 = jnp.max(sel3, axis=1, keepdims=True)
    i1 = jnp.min(jnp.where(sel3 == m1, pos3, gsz), axis=1, keepdims=True)
    m2 = jnp.max(jnp.where(pos3 == i1, neg, sel3), axis=1, keepdims=True)
    gs = (m1 + m2).reshape(N_GROUPS, tm)

    gidx = lax.broadcasted_iota(I32, (N_GROUPS, tm), 0)
    beaten = jnp.zeros((N_GROUPS, tm), I32)
    for gp in range(N_GROUPS):
        other = gs[gp:gp + 1, :]
        beats = (other > gs) | ((other == gs) & (gp < gidx))
        beaten = beaten + beats.astype(I32)
    gkeep = (beaten < TOPK_GROUPS).reshape(N_GROUPS, 1, tm)
    cur = jnp.where(gkeep, sel3, neg).reshape(ne, tm)

    eidx = lax.broadcasted_iota(I32, (ne, tm), 0)
    picked = jnp.zeros((ne, tm), jnp.bool_)
    idx_rows, w_rows = [], []
    for _ in range(TOP_K):
        m = jnp.max(cur, axis=0, keepdims=True)
        ik = jnp.min(jnp.where(cur == m, eidx, ne), axis=0, keepdims=True)
        hit = eidx == ik
        w_rows.append(jnp.sum(jnp.where(hit, scores, 0.0), axis=0, keepdims=True))
        idx_rows.append(ik)
        cur = jnp.where(hit, neg, cur)
        picked = picked | hit
    w = jnp.concatenate(w_rows, axis=0)
    idx_ref[...] = jnp.concatenate(idx_rows, axis=0)
    w_ref[...] = w / jnp.sum(w, axis=0, keepdims=True) * ROUTED_SCALE

    pf = picked.astype(F32)
    ta = lax.broadcasted_iota(I32, (tm, tm), 0)
    tb = lax.broadcasted_iota(I32, (tm, tm), 1)
    before = (ta < tb).astype(BF16)
    cnt = _dot(pf.astype(BF16), before) + carry_ref[...]
    rank_rows = [jnp.sum(jnp.where(eidx == ik, cnt, 0.0), axis=0, keepdims=True) for ik in idx_rows]
    rank_ref[...] = jnp.concatenate(rank_rows, axis=0).astype(I32)
    total = carry_ref[...] + jnp.sum(pf, axis=1, keepdims=True)
    carry_ref[...] = total
    cnt_ref[...] = jnp.broadcast_to(total, cnt_ref.shape).astype(I32)


def _route(h2, wr_t, bias_col, tm):
    t, d = h2.shape
    ne = wr_t.shape[0]
    row = lambda: pl.BlockSpec((TOP_K, tm), lambda i: (0, i))
    return pl.pallas_call(
        _route_kernel,
        grid=(t // tm,),
        in_specs=[pl.BlockSpec((tm, d), lambda i: (i, 0)),
                  pl.BlockSpec((ne, d), lambda i: (0, 0)),
                  pl.BlockSpec((ne, 1), lambda i: (0, 0))],
        out_specs=[row(), row(), row(), pl.BlockSpec((ne, 128), lambda i: (0, 0))],
        out_shape=[jax.ShapeDtypeStruct((TOP_K, t), I32), jax.ShapeDtypeStruct((TOP_K, t), F32),
                   jax.ShapeDtypeStruct((TOP_K, t), I32), jax.ShapeDtypeStruct((ne, 128), I32)],
        scratch_shapes=[pltpu.VMEM((ne, 1), F32)],
        compiler_params=_params(("arbitrary",), 40),
    )(h2, wr_t, bias_col)


def _slots_kernel(ps_ref, idx_ref, rank_ref, o_ref):
    idx = idx_ref[...]

    def body(e, acc):
        return jnp.where(idx == e, ps_ref[e], acc)

    row = lax.fori_loop(0, ps_ref.shape[0], body, jnp.zeros(idx.shape, I32)) + rank_ref[...]
    o_ref[...] = row * ROW_SUBLANES


def _slots(pstarts, idx, rank, tm):
    k, t = idx.shape
    blk = lambda: pl.BlockSpec((k, tm), lambda i, ps: (0, i))
    return pl.pallas_call(
        _slots_kernel,
        grid_spec=pltpu.PrefetchScalarGridSpec(
            num_scalar_prefetch=1, grid=(t // tm,), in_specs=[blk(), blk()], out_specs=blk()),
        out_shape=jax.ShapeDtypeStruct((k, t), I32),
        compiler_params=_params(("arbitrary",), 32),
    )(pstarts, idx, rank)


def _pad_bits():
    bit = EXPERT_ROWS // 2
    while bit >= 1:
        yield bit
        bit //= 2


def _row_tile(ref, first_sublane_row, rows=1):
    return ref.at[pl.ds(pl.multiple_of(first_sublane_row, ROW_SUBLANES), rows * ROW_SUBLANES), :]


def _dispatch_kernel(cnt_ref, ps_ref, pc_ref, dest_ref, h_ref, xs_ref, zero_ref, sem, pad_sem):
    tm = h_ref.shape[0] // ROW_SUBLANES

    def copy(t, k):
        return pltpu.make_async_copy(_row_tile(h_ref, t * ROW_SUBLANES),
                                     _row_tile(xs_ref, dest_ref[t * TOP_K + k]), sem)

    def start(t, carry):
        for k in range(TOP_K):
            copy(t, k).start(priority=k % 2)
        return carry

    def wait(t, carry):
        for k in range(TOP_K):
            copy(t, k).wait()
        return carry

    lax.fori_loop(0, tm, start, 0)

    @pl.when(pl.program_id(0) == 0)
    def _():
        zero_ref[...] = jnp.zeros_like(zero_ref)

        def pad_copy(first, rows):
            return pltpu.make_async_copy(_row_tile(zero_ref, 0, rows),
                                         _row_tile(xs_ref, first * ROW_SUBLANES, rows), pad_sem)

        def pads(e, fn):
            pad = pc_ref[e] - cnt_ref[e]
            lo = ps_ref[e] + cnt_ref[e]
            for bit in _pad_bits():
                @pl.when((pad & bit) != 0)
                def _():
                    fn(pad_copy(lo + (pad // (2 * bit)) * (2 * bit), bit))

        def start_pads(e, carry):
            pads(e, lambda cp: cp.start())
            return carry

        def wait_pads(e, carry):
            pads(e, lambda cp: cp.wait())
            return carry

        lax.fori_loop(0, cnt_ref.shape[0], start_pads, 0)
        lax.fori_loop(0, cnt_ref.shape[0], wait_pads, 0)

    lax.fori_loop(0, tm, wait, 0)


def _dispatch(counts, pstarts, pcounts, dest, h2p, n_rows, tm):
    t = h2p.shape[0] // ROW_SUBLANES
    return pl.pallas_call(
        _dispatch_kernel,
        grid_spec=pltpu.PrefetchScalarGridSpec(
            num_scalar_prefetch=3,
            grid=(t // tm,),
            in_specs=[pl.BlockSpec((tm * TOP_K,), lambda i, *_: (i,), memory_space=pltpu.SMEM),
                      pl.BlockSpec((tm * ROW_SUBLANES, 128), lambda i, *_: (i, 0))],
            out_specs=pl.BlockSpec(memory_space=pl.ANY),
            scratch_shapes=[pltpu.VMEM((EXPERT_ROWS // 2 * ROW_SUBLANES, 128), I32),
                            pltpu.SemaphoreType.DMA, pltpu.SemaphoreType.DMA],
        ),
        out_shape=jax.ShapeDtypeStruct((n_rows * ROW_SUBLANES, 128), I32),
        compiler_params=_params(("arbitrary",), 32),
    )(counts, pstarts, pcounts, dest, h2p)


def _expert_kernel(be_ref, rows_ref, nu_ref, first_ref, slot_ref, ahead_ref, warm_ref, x_ref, wg_hbm, wu_hbm, wd_hbm,
                   o_ref, wgf, wuf, wdf, wgb, wub, wdb, sem):
    nb = pl.program_id(0)
    half_f = wd_hbm.shape[1] // 2

    def fetch(e, s):
        lo, hi = pl.ds(0, half_f), pl.ds(half_f, half_f)
        return (pltpu.make_async_copy(wg_hbm.at[e], wgf.at[s], sem.at[s, 0]),
                pltpu.make_async_copy(wu_hbm.at[e], wuf.at[s], sem.at[s, 1]),
                pltpu.make_async_copy(wd_hbm.at[e, lo], wdf.at[s, lo], sem.at[s, 2]),
                pltpu.make_async_copy(wd_hbm.at[e, hi], wdf.at[s, hi], sem.at[s, 3]))

    def start(e, s):
        for i, cp in enumerate(fetch(e, s)):
            cp.start(priority=i % 2)

    @pl.when(nb == 0)
    def _():
        for i in range(WEIGHT_SLOTS - 1):
            @pl.when(warm_ref[i] >= 0)
            def _():
                start(warm_ref[i], i)

    @pl.when(first_ref[nb] == 1)
    def _():
        s = slot_ref[nb]
        for cp in fetch(be_ref[nb], s):
            cp.wait()

        @pl.when(ahead_ref[nb] >= 0)
        def _():
            start(ahead_ref[nb], (s + WEIGHT_SLOTS - 1) % WEIGHT_SLOTS)

    def mlp(rows, opening):
        if opening:
            s = slot_ref[nb]
            wg, wu, wd = wgf[s].astype(BF16), wuf[s].astype(BF16), wdf[s].astype(BF16)
            wgb[...], wub[...], wdb[...] = wg, wu, wd
        else:
            wg, wu, wd = wgb[...], wub[...], wdb[...]
        x = _unpack_halves(_load_row_tiles(x_ref, rows)).astype(BF16)
        g = _dot(x, wg)
        u = _dot(x, wu)
        a = (g * _sigmoid(g)) * u
        _store_row_tiles(o_ref, _pack_halves(_dot(a.astype(BF16), wd)))
        if rows < EXPERT_ROWS:
            o_ref[rows * ROW_SUBLANES:, :] = jnp.zeros(((EXPERT_ROWS - rows) * ROW_SUBLANES, 128), I32)

    half = EXPERT_ROWS // 2
    n_real = rows_ref[nb]
    for opening in (True, False):
        is_kind = (first_ref[nb] == 1) == opening
        pl.when(is_kind & (n_real > half))(functools.partial(mlp, EXPERT_ROWS, opening))
        pl.when(is_kind & (n_real > 0) & (n_real <= half))(functools.partial(mlp, half, opening))


def _experts(block_e, rows, n_used, first, slot, ahead_e, warm_e, xs, w_gate, w_up, w_down):
    r, dp = xs.shape
    ne, d, f = w_gate.shape
    bm = EXPERT_ROWS * ROW_SUBLANES
    ws = WEIGHT_SLOTS
    row_blk = lambda nb, be, rw, nu, *_: (jnp.minimum(nb, nu[0] - 1), 0)
    return pl.pallas_call(
        _expert_kernel,
        grid_spec=pltpu.PrefetchScalarGridSpec(
            num_scalar_prefetch=7,
            grid=(r // bm,),
            in_specs=[pl.BlockSpec((bm, dp), row_blk),
                      pl.BlockSpec(memory_space=pl.ANY),
                      pl.BlockSpec(memory_space=pl.ANY),
                      pl.BlockSpec(memory_space=pl.ANY)],
            out_specs=pl.BlockSpec((bm, dp), row_blk),
            scratch_shapes=[pltpu.VMEM((ws, d, f), F32), pltpu.VMEM((ws, d, f), F32), pltpu.VMEM((ws, f, d), F32),
                            pltpu.VMEM((d, f), BF16), pltpu.VMEM((d, f), BF16), pltpu.VMEM((f, d), BF16),
                            pltpu.SemaphoreType.DMA((ws, 4))],
        ),
        out_shape=jax.ShapeDtypeStruct((r, dp), I32),
        compiler_params=_params(("arbitrary",), 60),
    )(block_e, rows, n_used, first, slot, ahead_e, warm_e, xs, w_gate, w_up, w_down)


def _combine_kernel(dest_ref, dnext_ref, w_ref, x1_ref, h_ref, gt_ref, fg_ref, wg_ref, wu_ref, wd_ref, ys_ref,
                    o_ref, buf0, buf1, routed_ref, sem):
    buf = (buf0, buf1)
    i = pl.program_id(0)
    last = pl.num_programs(0) - 1
    tm = x1_ref.shape[0]

    def copy(idx_ref, s, t, k):
        return pltpu.make_async_copy(_row_tile(ys_ref, idx_ref[t * TOP_K + k]),
                                     _row_tile(buf[s].at[k], t * ROW_SUBLANES), sem.at[s])

    def start_token(idx_ref, s, t):
        for k in range(TOP_K):
            copy(idx_ref, s, t, k).start(priority=k % 2)

    def wait_tile(s):
        def body(t, carry):
            for k in range(TOP_K):
                copy(dest_ref, s, t, k).wait()
            return carry
        lax.fori_loop(0, tm, body, 0)

    @pl.when(i == 0)
    def _():
        def body(t, carry):
            start_token(dest_ref, 0, t)
            return carry
        lax.fori_loop(0, tm, body, 0)

    def gather_reduce(slot):
        wait_tile(slot)

        def group(g, carry):
            for r in range(8):
                start_token(dnext_ref, 1 - slot, g * 8 + r)
            base = pl.multiple_of(g * (8 * ROW_SUBLANES), 8 * ROW_SUBLANES)
            w8 = w_ref[pl.ds(pl.multiple_of(g * 8, 8), 8), :]
            lo = [None] * ROW_SUBLANES
            hi = [None] * ROW_SUBLANES
            for k in range(TOP_K):
                wk = w8[:, k:k + 1]
                for s in range(ROW_SUBLANES):
                    p = buf[slot][k, pl.ds(base + s, 8, stride=ROW_SUBLANES), :]
                    l = lax.bitcast_convert_type(lax.shift_left(p, 16), F32) * wk
                    h = lax.bitcast_convert_type(p & -65536, F32) * wk
                    lo[s] = l if k == 0 else lo[s] + l
                    hi[s] = h if k == 0 else hi[s] + h
            routed_ref[pl.ds(pl.multiple_of(g * 8, 8), 8), :] = jnp.concatenate(lo + hi, axis=1)
            return carry

        lax.fori_loop(0, tm // 8, group, 0)

    for slot in range(2):
        pl.when(i % 2 == slot)(functools.partial(gather_reduce, slot))

    x = h_ref[...]
    g = _dot(x, wg_ref[...])
    u = _dot(x, wu_ref[...])
    acc = _dot(((g * _sigmoid(g)) * u).astype(BF16), wd_ref[...])
    y = x1_ref[...] + gt_ref[0] * (routed_ref[...] + acc)
    ms = jnp.mean(y * y, axis=-1, keepdims=True)
    o_ref[...] = y * lax.rsqrt(ms + RMS_EPS) * fg_ref[...]

    for slot in range(2):
        pl.when((i == last) & (i % 2 == slot))(functools.partial(wait_tile, 1 - slot))


def _combine(dest, w_tok, x1, h2, gt2, fg, wg, wu, wd, ys, seq, tm):
    t, d = x1.shape
    f = wg.shape[1]
    nt = t // tm
    per_b = seq // tm
    const = lambda shape: pl.BlockSpec(shape, lambda i: (0, 0))
    return pl.pallas_call(
        _combine_kernel,
        grid=(nt,),
        in_specs=[pl.BlockSpec((tm * TOP_K,), lambda i: (i,), memory_space=pltpu.SMEM),
                  pl.BlockSpec((tm * TOP_K,), lambda i: (jnp.minimum(i + 1, nt - 1),), memory_space=pltpu.SMEM),
                  pl.BlockSpec((tm, TOP_K), lambda i: (i, 0)),
                  pl.BlockSpec((tm, d), lambda i: (i, 0)),
                  pl.BlockSpec((tm, d), lambda i: (i, 0)),
                  pl.BlockSpec((1, 1, d), lambda i: (i // per_b, 0, 0)),
                  const((1, d)), const((d, f)), const((d, f)), const((f, d)),
                  pl.BlockSpec(memory_space=pl.ANY)],
        out_specs=pl.BlockSpec((tm, d), lambda i: (i, 0)),
        out_shape=jax.ShapeDtypeStruct((t, d), F32),
        scratch_shapes=[pltpu.VMEM((TOP_K, tm * ROW_SUBLANES, 128), I32),
                        pltpu.VMEM((TOP_K, tm * ROW_SUBLANES, 128), I32), pltpu.VMEM((tm, d), F32),
                        pltpu.SemaphoreType.DMA((2,))],
        compiler_params=_params(("arbitrary",), 56),
    )(dest, dest, w_tok, x1, h2, gt2, fg, wg, wu, wd, ys)


def _block_tables(pends, row_ends, n_blocks):
    bm = EXPERT_ROWS
    n_used = (pends[-1] // bm).astype(I32)
    j = jnp.arange(n_blocks, dtype=I32)
    jc = jnp.minimum(j, n_used - 1)
    block_e = jnp.sum((pends[None, :] <= (jc * bm)[:, None]).astype(I32), axis=1)
    rows = jnp.where(j < n_used, jnp.clip(row_ends[block_e] - j * bm, 0, bm), 0).astype(I32)
    prev_e = jnp.concatenate([jnp.full((1,), -1, I32), block_e[:-1]])
    first = ((block_e != prev_e) & (j < n_used)).astype(I32)
    slot = (jnp.cumsum(first) - 1) % WEIGHT_SLOTS
    big = jnp.int32(n_blocks)
    later = lax.cummin(jnp.where(first == 1, j, big)[::-1])[::-1]
    nxt = jnp.concatenate([later[1:], big[None]])
    hop = lambda at: jnp.where(at < big, nxt[jnp.minimum(at, n_blocks - 1)], big)
    expert_at = lambda at: jnp.where(at < big, block_e[jnp.minimum(at, n_blocks - 1)], -1).astype(I32)
    ahead = j
    warm = []
    for _ in range(WEIGHT_SLOTS - 1):
        warm.append(expert_at(ahead[0]))
        ahead = hop(ahead)
    return block_e, rows, n_used[None], first, slot.astype(I32), expert_at(ahead), jnp.stack(warm)


def kernel(x, c, w_ada, b_ada, norm1_g, w_in, conv_w, conv_b, lru_wa, lru_ba, lru_wx, lru_bx,
           lru_lambda, hgrn_lb, hgrn_norm_g, w_out, norm2_g, w_router, router_bias,
           w_gate, w_up, w_down, ws_gate, ws_up, ws_down, final_g):
    batch, seq, d = x.shape
    t = batch * seq
    depth = w_ada.shape[0]
    d_lru = conv_w.shape[2]
    d_hgrn = d - d_lru
    heads = d_hgrn // HEAD_DIM
    ne = w_router.shape[2]
    assert d_lru % LRU_BLOCK == 0 and d_hgrn % HEAD_DIM == 0 and seq % 512 == 0 and ne % (8 * N_GROUPS) == 0
    assert depth == 1, "the final norm is fused into the combine kernel"
    assert d == 2 * 128 * ROW_SUBLANES, "a packed row must be exactly one (ROW_SUBLANES, 128) tile"
    l = 0

    lower_bounds = jnp.cumsum(jax.nn.softmax(hgrn_lb.astype(F32), axis=0), axis=0)
    c_pad = jnp.pad(c, ((0, -batch % 8), (0, 0)))
    mod = _adaln_mod(c_pad, w_ada[l], b_ada[l][None, :])[:batch]
    sh1, sc1, gt1, sh2, sc2, gt2 = [m[:, None, :] for m in jnp.split(mod, 6, axis=-1)]

    proj = _norm_matmul(x, norm1_g[l][None, :], sc1, sh1, w_in[l].astype(BF16), min(1024, seq), 2048, BF16)
    y_lru = _rg_lru(proj, conv_w[l], conv_b[l][None, :], lru_wa[l].astype(BF16), lru_ba[l][None, :],
                    lru_wx[l].astype(BF16), lru_bx[l][None, :], lru_lambda[l][None, :], batch, seq, 256)
    y_hgrn = _hgrn2(proj, lower_bounds[l][None, :], hgrn_norm_g[l][None, :], batch, seq, heads,
                    2 * d_lru // HEAD_DIM, min(seq, 2048), 4)
    wo = w_out[l].astype(BF16)
    x1, h2, h2p = _out_proj(y_lru, y_hgrn, wo[:d_lru], wo[d_lru:], x, gt1, norm2_g[l][None, :], sc2, sh2, 512)

    idx, w_sel, rank, counts = _route(h2, w_router[l].T.astype(BF16), router_bias[l][:, None], 512)
    counts = counts[:, 0]
    bm = EXPERT_ROWS
    pcounts = (counts + bm - 1) // bm * bm
    pends = jnp.cumsum(pcounts)
    pstarts = pends - pcounts
    n_rows = -(-(t * TOP_K + ne * (bm - 1)) // bm) * bm
    dest = _slots(pstarts, idx, rank, min(2048, t)).T.reshape(-1)
    xs = _dispatch(counts, pstarts, pcounts, dest, h2p, n_rows, 256)
    ys = _experts(*_block_tables(pends, pstarts + counts, n_rows // bm), xs, w_gate[l], w_up[l], w_down[l])
    out = _combine(dest, w_sel.T, x1, h2, gt2, final_g[None, :], ws_gate[l].astype(BF16),
                   ws_up[l].astype(BF16), ws_down[l].astype(BF16), ys, seq, 256)
    return out.reshape(batch, seq, d)
```

```python
import functools

import jax
import jax.numpy as jnp
from jax import lax
from jax.experimental import pallas as pl
from jax.experimental.pallas import tpu as pltpu

F32 = jnp.float32
BF16 = jnp.bfloat16
I32 = jnp.int32

RMS_EPS = 1e-6
LRU_C = 8.0
LRU_BLOCK = 128
CONV_WIDTH = 4
HEAD_DIM = 128
CHUNK = 64
SUB = 16
N_GROUPS = 8
TOPK_GROUPS = 4
TOP_K = 8
ROUTED_SCALE = 2.5
EXPERT_ROWS = 256
ROW_SUBLANES = 8
WEIGHT_SLOTS = 3
MIB = 1024 * 1024


def _params(sem, vmem_mib):
    return pltpu.CompilerParams(dimension_semantics=sem, vmem_limit_bytes=vmem_mib * MIB)


def _sigmoid(x):
    return 0.5 * jnp.tanh(0.5 * x) + 0.5


def _dot(a, b):
    return jnp.dot(a, b, preferred_element_type=F32)


def _dot_nt(a, b):
    return lax.dot_general(a, b, (((1,), (1,)), ((), ())), preferred_element_type=F32)


def _dot_tn(a, b):
    return lax.dot_general(a, b, (((0,), (0,)), ((), ())), preferred_element_type=F32)


def _mod_kernel(c_ref, w_ref, b_ref, o_ref):
    c = c_ref[...]
    cond = c * _sigmoid(c)
    o_ref[...] = _dot(cond.astype(BF16), w_ref[...].astype(BF16)) + b_ref[...]


def _adaln_mod(c_pad, w_ada, b_ada):
    rows, d = c_pad.shape
    n = w_ada.shape[1]
    bn = 1024
    return pl.pallas_call(
        _mod_kernel,
        grid=(n // bn,),
        in_specs=[pl.BlockSpec((rows, d), lambda j: (0, 0)),
                  pl.BlockSpec((d, bn), lambda j: (0, j)),
                  pl.BlockSpec((1, bn), lambda j: (0, j))],
        out_specs=pl.BlockSpec((rows, bn), lambda j: (0, j)),
        out_shape=jax.ShapeDtypeStruct((rows, n), F32),
        compiler_params=_params(("arbitrary",), 40),
    )(c_pad, w_ada, b_ada)


def _norm_mm_kernel(x_ref, g_ref, sc_ref, sh_ref, w_ref, o_ref):
    x = x_ref[0]
    ms = jnp.mean(x * x, axis=-1, keepdims=True)
    y = x * lax.rsqrt(ms + RMS_EPS) * g_ref[...]
    h = (y * (1.0 + sc_ref[0]) + sh_ref[0]).astype(BF16)
    o_ref[...] = _dot(h, w_ref[...]).astype(o_ref.dtype)


def _norm_matmul(x, g, sc, sh, w, bm, bn, out_dtype):
    b, l, k = x.shape
    n = w.shape[1]
    nt = l // bm
    bvec = lambda: pl.BlockSpec((1, 1, k), lambda j, i, t: (i, 0, 0))
    return pl.pallas_call(
        _norm_mm_kernel,
        grid=(n // bn, b, nt),
        in_specs=[pl.BlockSpec((1, bm, k), lambda j, i, t: (i, t, 0)),
                  pl.BlockSpec((1, k), lambda j, i, t: (0, 0)),
                  bvec(), bvec(),
                  pl.BlockSpec((k, bn), lambda j, i, t: (0, j))],
        out_specs=pl.BlockSpec((bm, bn), lambda j, i, t: (i * nt + t, j)),
        out_shape=jax.ShapeDtypeStruct((b * l, n), out_dtype),
        compiler_params=_params(("arbitrary", "arbitrary", "arbitrary"), 56),
    )(x, g, sc, sh, w)


def _shift_rows(u, prev, k):
    if k == 0:
        return u
    tt, c = u.shape
    rot = pltpu.roll(u.reshape(tt // 8, 8, c), k, 1)
    before = jnp.concatenate([pltpu.roll(prev, k, 0)[None], rot[:-1]], axis=0)
    r = lax.broadcasted_iota(I32, rot.shape, 1)
    return jnp.where(r < k, before, rot).reshape(tt, c)


def _lru_kernel(u_ref, z_ref, cw_ref, cb_ref, wa_ref, ba_ref, wx_ref, bx_ref, lam_ref,
                o_ref, prev_ref, h_ref):
    j = pl.program_id(1)
    tt, c = u_ref.shape

    @pl.when(j == 0)
    def _():
        prev_ref[...] = jnp.zeros_like(prev_ref)
        h_ref[...] = jnp.zeros_like(h_ref)

    u = u_ref[...].astype(F32)
    prev = prev_ref[...]
    xc = jnp.broadcast_to(cb_ref[...], (tt, c))
    for w in range(CONV_WIDTH):
        xc = xc + cw_ref[w:w + 1, :] * _shift_rows(u, prev, CONV_WIDTH - 1 - w)
    prev_ref[...] = u[tt - 8:, :]

    xcb = xc.astype(BF16)
    ra, rx = [], []
    for blk in range(c // LRU_BLOCK):
        xs = xcb[:, blk * LRU_BLOCK:(blk + 1) * LRU_BLOCK]
        ra.append(_dot(xs, wa_ref[blk]))
        rx.append(_dot(xs, wx_ref[blk]))
    r = _sigmoid(jnp.concatenate(ra, axis=1) + ba_ref[...])
    ig = _sigmoid(jnp.concatenate(rx, axis=1) + bx_ref[...])

    nl = -lam_ref[...]
    softplus = jnp.maximum(nl, 0.0) + jnp.log1p(jnp.exp(-jnp.abs(nl)))
    log_a = (-LRU_C) * r * softplus
    a = jnp.exp(log_a)
    mult = jnp.sqrt(1.0 - jnp.exp(2.0 * log_a))
    row = lax.broadcasted_iota(I32, (tt, c), 0)
    mult = jnp.where((row == 0) & (j == 0), 1.0, mult)
    bv = mult * ig * xc

    a3 = a.reshape(tt // 8, 8, c)
    b3 = bv.reshape(tt // 8, 8, c)
    in_group = lax.broadcasted_iota(I32, a3.shape, 1)
    s = 1
    while s < 8:
        keep = in_group >= s
        b3 = jnp.where(keep, a3 * pltpu.roll(b3, s, 1) + b3, b3)
        a3 = jnp.where(keep, a3 * pltpu.roll(a3, s, 1), a3)
        s *= 2
    carry = h_ref[...]
    groups = []
    for gi in range(tt // 8):
        hg = b3[gi] + a3[gi] * carry
        carry = hg[7:8, :]
        groups.append(hg)
    h = jnp.concatenate(groups, axis=0)
    h_ref[...] = carry

    z = z_ref[...].astype(F32)
    gelu = 0.5 * z * (1.0 + jnp.tanh(0.7978845608028654 * (z + 0.044715 * (z * z * z))))
    o_ref[...] = (h * gelu).astype(o_ref.dtype)


def _rg_lru(proj, conv_w, conv_b, wa, ba, wx, bx, lam, batch, seq, tt):
    t = proj.shape[0]
    c = conv_w.shape[1]
    nt = seq // tt
    vec = lambda: pl.BlockSpec((1, c), lambda i, j: (0, 0))
    mat = lambda: pl.BlockSpec(wa.shape, lambda i, j: (0, 0, 0))
    return pl.pallas_call(
        _lru_kernel,
        grid=(batch, nt),
        in_specs=[pl.BlockSpec((tt, c), lambda i, j: (i * nt + j, 0)),
                  pl.BlockSpec((tt, c), lambda i, j: (i * nt + j, 1)),
                  pl.BlockSpec((CONV_WIDTH, c), lambda i, j: (0, 0)),
                  vec(), mat(), vec(), mat(), vec(), vec()],
        out_specs=pl.BlockSpec((tt, c), lambda i, j: (i * nt + j, 0)),
        out_shape=jax.ShapeDtypeStruct((t, c), BF16),
        scratch_shapes=[pltpu.VMEM((8, c), F32), pltpu.VMEM((1, c), F32)],
        compiler_params=_params(("arbitrary", "arbitrary"), 48),
    )(proj, proj, conv_w, conv_b, wa, ba, wx, bx, lam)


def _hgrn_lanes(wid):
    return [slice(h * HEAD_DIM, (h + 1) * HEAD_DIM) for h in range(wid // HEAD_DIM)]


def _per_head(wid, fn):
    return jnp.concatenate([fn(h, ln) for h, ln in enumerate(_hgrn_lanes(wid))], axis=1)


def _row_on_sublanes(slab_ref, r):
    return jnp.concatenate([slab_ref[h, pl.ds(r, 8, stride=0), :] for h in range(slab_ref.shape[0])], axis=1)


_N_SUB = CHUNK // SUB
_STACK_ROWS = 8 * 16 + 8 * 8
_KT_OFFSET = [SUB * i * (i - 1) // 2 for i in range(_N_SUB + 1)]


def _hgrn_front_shapes(wid):
    return [pltpu.VMEM((CHUNK, wid), BF16),
            pltpu.VMEM((CHUNK, wid), BF16),
            pltpu.VMEM((CHUNK, wid), BF16),
            pltpu.VMEM((CHUNK, wid), BF16),
            pltpu.VMEM((1, wid), F32),
            pltpu.VMEM((CHUNK, wid), F32),
            pltpu.VMEM((wid // HEAD_DIM, CHUNK, HEAD_DIM), F32),
            pltpu.VMEM((wid // HEAD_DIM, CHUNK, HEAD_DIM), F32),
            pltpu.VMEM((_KT_OFFSET[_N_SUB], wid), BF16),
            pltpu.VMEM((_N_SUB * _STACK_ROWS, wid), BF16)]


def _hgrn_front(qi, fi, v, gi, lb, out):
    qe_ref, qt_ref, vb_ref, kd_ref, decay_ref, gate_ref, v_ref, c_ref, kt_ref, stack_ref = out
    n, wid = qi.shape
    q = qi * _sigmoid(qi)
    f = lb + (1.0 - lb) * _sigmoid(fi)
    row = lax.broadcasted_iota(I32, (n, wid), 0)
    b = jnp.log2(f)
    s = 1
    while s < n:
        b = b + jnp.where(row >= s, pltpu.roll(b, s, 0), 0.0)
        s *= 2
    c = b - jnp.log2(1.0 - f)
    qe_ref[...] = (q * jnp.exp2(b)).astype(BF16)
    refb = jnp.concatenate(
        [jnp.broadcast_to(b[i * SUB:i * SUB + 1, :], (SUB, wid)) for i in range(_N_SUB)], axis=0)
    qt_ref[...] = (q * jnp.exp2(b - refb)).astype(BF16)
    vb_ref[...] = v.astype(BF16)
    for h, ln in enumerate(_hgrn_lanes(wid)):
        v_ref[h] = v[:, ln]
        c_ref[h] = c[:, ln]
    gate_ref[...] = gi * _sigmoid(gi)
    b_last = b[n - 1:n, :]
    kd_ref[...] = jnp.exp2(b_last - c).astype(BF16)
    decay_ref[...] = jnp.exp2(b_last)
    row8 = lax.broadcasted_iota(I32, (8, wid), 0)
    for i in range(_N_SUB):
        lo = i * SUB
        if i > 0:
            kt_ref[_KT_OFFSET[i]:_KT_OFFSET[i + 1], :] = jnp.exp2(b[lo:lo + 1, :] - c[:lo, :]).astype(BF16)
        q0, q1 = q[lo:lo + 8, :], q[lo + 8:lo + SUB, :]
        b0, b1 = b[lo:lo + 8, :], b[lo + 8:lo + SUB, :]
        base = i * _STACK_ROWS
        pending = None
        for s_ in range(SUB):
            cs = _row_on_sublanes(c_ref, lo + s_)
            p1 = q1 * jnp.exp2(b1 - cs)
            if s_ < 8:
                pair = [jnp.where(row8 >= s_, q0 * jnp.exp2(b0 - cs), 0.0), p1]
                at = base + 16 * s_
            elif pending is None:
                pending = jnp.where(row8 >= s_ - 8, p1, 0.0)
                continue
            else:
                pair = [pending, jnp.where(row8 >= s_ - 8, p1, 0.0)]
                pending = None
                at = base + 64 + 8 * (s_ - 1)
            stack_ref[at:at + 16, :] = jnp.concatenate(pair, axis=0).astype(BF16)


def _hgrn_products(front, states, ones_b):
    qe_ref, qt_ref, _, _, _, _, _, _, kt_ref, stack_ref = front
    wid = qe_ref.shape[1]
    lanes = _hgrn_lanes(wid)
    o_state = _per_head(wid, lambda h, ln: _dot_nt(qe_ref[:, ln], states[h].astype(BF16)))
    scores = [[_dot_nt(qt_ref[i * SUB:(i + 1) * SUB, ln], kt_ref[_KT_OFFSET[i]:_KT_OFFSET[i + 1], ln]).astype(BF16)
               for ln in lanes] for i in range(1, _N_SUB)]
    reds = [_per_head(wid, lambda h, ln: _dot(stack_ref[i * _STACK_ROWS:(i + 1) * _STACK_ROWS, ln], ones_b))
            for i in range(_N_SUB)]
    return o_state, scores, reds


def _hgrn_finish(front, products, states, ng):
    _, _, vb_ref, kd_ref, decay_ref, gate_ref, v_ref, _, _, _ = front
    o_state, scores, reds = products
    n, wid = vb_ref.shape
    outs = []
    for i in range(_N_SUB):
        lo = i * SUB
        acc = o_state[lo:lo + SUB, :]
        if i > 0:
            sc = scores[i - 1]
            acc = acc + _per_head(wid, lambda h, ln: _dot(sc[h], vb_ref[:lo, ln]))
        red = reds[i]
        acc0 = jnp.zeros((8, wid), F32)
        acc1 = jnp.zeros((8, wid), F32)
        for s_ in range(SUB):
            vs = _row_on_sublanes(v_ref, lo + s_)
            if s_ < 8:
                acc0 = acc0 + red[16 * s_:16 * s_ + 8, :] * vs
                acc1 = acc1 + red[16 * s_ + 8:16 * s_ + 16, :] * vs
            else:
                acc1 = acc1 + red[64 + 8 * s_:72 + 8 * s_, :] * vs
        outs.append(acc + jnp.concatenate([acc0, acc1], axis=0))
    o = jnp.concatenate(outs, axis=0)
    decay = decay_ref[...]
    new_states = [states[h] * decay[:, ln] + _dot_tn(vb_ref[:, ln], kd_ref[:, ln])
                  for h, ln in enumerate(_hgrn_lanes(wid))]
    oo = o * o
    scale = _per_head(wid, lambda h, ln: jnp.broadcast_to(
        lax.rsqrt(jnp.mean(oo[:, ln], axis=-1, keepdims=True) + RMS_EPS), (n, HEAD_DIM)) * ng)
    return o * scale * gate_ref[...], new_states


def _hgrn_kernel(q_ref, f_ref, v_ref, g_ref, lb_ref, ng_ref, o_ref, st_ref, *front_refs):
    @pl.when(pl.program_id(2) == 0)
    def _():
        st_ref[...] = jnp.zeros_like(st_ref)

    lb = lb_ref[...]
    ng = ng_ref[...]
    heads = q_ref.shape[1] // HEAD_DIM
    n_chunks = q_ref.shape[0] // CHUNK
    ones_b = jnp.ones((HEAD_DIM, HEAD_DIM), BF16)
    half = len(front_refs) // 2
    fronts = (front_refs[:half], front_refs[half:])

    def chunk_rows(ci):
        return pl.ds(pl.multiple_of(ci * CHUNK, CHUNK), CHUNK)

    def front(ci, out):
        rows = chunk_rows(ci)
        _hgrn_front(q_ref[rows, :].astype(F32), f_ref[rows, :].astype(F32),
                    v_ref[rows, :].astype(F32), g_ref[rows, :].astype(F32), lb, out)

    def back(ci, cur, nxt):
        states = [st_ref[h] for h in range(heads)]
        products = _hgrn_products(cur, states, ones_b)
        front(jnp.minimum(ci + 1, n_chunks - 1), nxt)
        y, new_states = _hgrn_finish(cur, products, states, ng)
        for h in range(heads):
            st_ref[h] = new_states[h]
        o_ref[chunk_rows(ci), :] = y.astype(o_ref.dtype)

    def body(pair, carry):
        back(2 * pair, fronts[0], fronts[1])
        back(2 * pair + 1, fronts[1], fronts[0])
        return carry

    front(0, fronts[0])
    lax.fori_loop(0, n_chunks // 2, body, 0)


def _hgrn2(proj, lb, norm_g, batch, seq, heads, col0, tt, hp):
    t = proj.shape[0]
    nt = seq // tt
    wid = hp * HEAD_DIM
    assert heads % hp == 0 and col0 % hp == 0
    col = lambda off: pl.BlockSpec((tt, wid), lambda i, h, j: (i * nt + j, (col0 + off * heads) // hp + h))
    return pl.pallas_call(
        _hgrn_kernel,
        grid=(batch, heads // hp, nt),
        in_specs=[col(0), col(1), col(2), col(3),
                  pl.BlockSpec((1, wid), lambda i, h, j: (0, h)),
                  pl.BlockSpec((1, HEAD_DIM), lambda i, h, j: (0, 0))],
        out_specs=pl.BlockSpec((tt, wid), lambda i, h, j: (i * nt + j, h)),
        out_shape=jax.ShapeDtypeStruct((t, heads * HEAD_DIM), BF16),
        scratch_shapes=[pltpu.VMEM((hp, HEAD_DIM, HEAD_DIM), F32)] + 2 * _hgrn_front_shapes(wid),
        compiler_params=_params(("arbitrary", "arbitrary", "arbitrary"), 48),
    )(proj, proj, proj, proj, lb, norm_g)


def _pack_halves(y):
    n = y.shape[1] // 2
    lo = lax.bitcast_convert_type(y[:, :n].astype(BF16).astype(F32), I32)
    hi = lax.bitcast_convert_type(y[:, n:].astype(BF16).astype(F32), I32)
    return (hi & -65536) | lax.shift_right_logical(lo, 16)


def _unpack_halves(p):
    lo = lax.bitcast_convert_type(lax.shift_left(p, 16), F32)
    hi = lax.bitcast_convert_type(p & -65536, F32)
    return jnp.concatenate([lo, hi], axis=1)


def _store_row_tiles(ref, packed):
    m = packed.shape[0]
    for s in range(ROW_SUBLANES):
        ref[pl.ds(s, m, stride=ROW_SUBLANES), :] = packed[:, s * 128:(s + 1) * 128]


def _load_row_tiles(ref, m):
    return jnp.concatenate([ref[pl.ds(s, m, stride=ROW_SUBLANES), :] for s in range(ROW_SUBLANES)], axis=1)


def _out_kernel(yl_ref, yh_ref, wl_ref, wh_ref, x_ref, gt_ref, g2_ref, sc_ref, sh_ref, x1_ref, h2_ref, h2p_ref):
    mix = _dot(yl_ref[...], wl_ref[...]) + _dot(yh_ref[...], wh_ref[...])
    x1 = x_ref[0] + gt_ref[0] * mix
    x1_ref[...] = x1
    ms = jnp.mean(x1 * x1, axis=-1, keepdims=True)
    y = x1 * lax.rsqrt(ms + RMS_EPS) * g2_ref[...]
    h2 = y * (1.0 + sc_ref[0]) + sh_ref[0]
    h2_ref[...] = h2.astype(BF16)
    _store_row_tiles(h2p_ref, _pack_halves(h2))


def _out_proj(yl, yh, wl, wh, x, gt1, g2, sc2, sh2, bm):
    b, l, d = x.shape
    t = b * l
    nt = l // bm
    cl, ch = yl.shape[1], yh.shape[1]
    bvec = lambda: pl.BlockSpec((1, 1, d), lambda i, j: (i, 0, 0))
    return pl.pallas_call(
        _out_kernel,
        grid=(b, nt),
        in_specs=[pl.BlockSpec((bm, cl), lambda i, j: (i * nt + j, 0)),
                  pl.BlockSpec((bm, ch), lambda i, j: (i * nt + j, 0)),
                  pl.BlockSpec((cl, d), lambda i, j: (0, 0)),
                  pl.BlockSpec((ch, d), lambda i, j: (0, 0)),
                  pl.BlockSpec((1, bm, d), lambda i, j: (i, j, 0)),
                  bvec(),
                  pl.BlockSpec((1, d), lambda i, j: (0, 0)),
                  bvec(), bvec()],
        out_specs=[pl.BlockSpec((bm, d), lambda i, j: (i * nt + j, 0)),
                   pl.BlockSpec((bm, d), lambda i, j: (i * nt + j, 0)),
                   pl.BlockSpec((bm * ROW_SUBLANES, 128), lambda i, j: (i * nt + j, 0))],
        out_shape=[jax.ShapeDtypeStruct((t, d), F32), jax.ShapeDtypeStruct((t, d), BF16),
                   jax.ShapeDtypeStruct((t * ROW_SUBLANES, 128), I32)],
        compiler_params=_params(("arbitrary", "arbitrary"), 56),
    )(yl, yh, wl, wh, x, gt1, g2, sc2, sh2)


def _route_kernel(h_ref, wr_ref, bias_ref, idx_ref, w_ref, rank_ref, cnt_ref, carry_ref):
    step = pl.program_id(0)
    tm = h_ref.shape[0]
    ne = wr_ref.shape[0]
    gsz = ne // N_GROUPS
    neg = -jnp.inf

    @pl.when(step == 0)
    def _():
        carry_ref[...] = jnp.zeros_like(carry_ref)

    logits = _dot_nt(wr_ref[...], h_ref[...])
    scores = _sigmoid(logits)
    sel = scores + bias_ref[...]

    sel3 = sel.reshape(N_GROUPS, gsz, tm)
    pos3 = lax.broadcasted_iota(I32, (N_GROUPS, gsz, tm), 1)
    m1 = jnp.max(sel3, axis=1, keepdims=True)
    i1 = jnp.min(jnp.where(sel3 == m1, pos3, gsz), axis=1, keepdims=True)
    m2 = jnp.max(jnp.where(pos3 == i1, neg, sel3), axis=1, keepdims=True)
    gs = (m1 + m2).reshape(N_GROUPS, tm)

    gidx = lax.broadcasted_iota(I32, (N_GROUPS, tm), 0)
    beaten = jnp.zeros((N_GROUPS, tm), I32)
    for gp in range(N_GROUPS):
        other = gs[gp:gp + 1, :]
        beats = (other > gs) | ((other == gs) & (gp < gidx))
        beaten = beaten + beats.astype(I32)
    gkeep = (beaten < TOPK_GROUPS).reshape(N_GROUPS, 1, tm)
    cur = jnp.where(gkeep, sel3, neg).reshape(ne, tm)

    eidx = lax.broadcasted_iota(I32, (ne, tm), 0)
    picked = jnp.zeros((ne, tm), jnp.bool_)
    idx_rows, w_rows = [], []
    for _ in range(TOP_K):
        m = jnp.max(cur, axis=0, keepdims=True)
        ik = jnp.min(jnp.where(cur == m, eidx, ne), axis=0, keepdims=True)
        hit = eidx == ik
        w_rows.append(jnp.sum(jnp.where(hit, scores, 0.0), axis=0, keepdims=True))
        idx_rows.append(ik)
        cur = jnp.where(hit, neg, cur)
        picked = picked | hit
    w = jnp.concatenate(w_rows, axis=0)
    idx_ref[...] = jnp.concatenate(idx_rows, axis=0)
    w_ref[...] = w / jnp.sum(w, axis=0, keepdims=True) * ROUTED_SCALE

    pf = picked.astype(F32)
    ta = lax.broadcasted_iota(I32, (tm, tm), 0)
    tb = lax.broadcasted_iota(I32, (tm, tm), 1)
    before = (ta < tb).astype(BF16)
    cnt = _dot(pf.astype(BF16), before) + carry_ref[...]
    rank_rows = [jnp.sum(jnp.where(eidx == ik, cnt, 0.0), axis=0, keepdims=True) for ik in idx_rows]
    rank_ref[...] = jnp.concatenate(rank_rows, axis=0).astype(I32)
    total = carry_ref[...] + jnp.sum(pf, axis=1, keepdims=True)
    carry_ref[...] = total
    cnt_ref[...] = jnp.broadcast_to(total, cnt_ref.shape).astype(I32)


def _route(h2, wr_t, bias_col, tm):
    t, d = h2.shape
    ne = wr_t.shape[0]
    row = lambda: pl.BlockSpec((TOP_K, tm), lambda i: (0, i))
    return pl.pallas_call(
        _route_kernel,
        grid=(t // tm,),
        in_specs=[pl.BlockSpec((tm, d), lambda i: (i, 0)),
                  pl.BlockSpec((ne, d), lambda i: (0, 0)),
                  pl.BlockSpec((ne, 1), lambda i: (0, 0))],
        out_specs=[row(), row(), row(), pl.BlockSpec((ne, 128), lambda i: (0, 0))],
        out_shape=[jax.ShapeDtypeStruct((TOP_K, t), I32), jax.ShapeDtypeStruct((TOP_K, t), F32),
                   jax.ShapeDtypeStruct((TOP_K, t), I32), jax.ShapeDtypeStruct((ne, 128), I32)],
        scratch_shapes=[pltpu.VMEM((ne, 1), F32)],
        compiler_params=_params(("arbitrary",), 40),
    )(h2, wr_t, bias_col)


def _slots_kernel(ps_ref, idx_ref, rank_ref, o_ref):
    idx = idx_ref[...]

    def body(e, acc):
        return jnp.where(idx == e, ps_ref[e], acc)

    row = lax.fori_loop(0, ps_ref.shape[0], body, jnp.zeros(idx.shape, I32)) + rank_ref[...]
    o_ref[...] = row * ROW_SUBLANES


def _slots(pstarts, idx, rank, tm):
    k, t = idx.shape
    blk = lambda: pl.BlockSpec((k, tm), lambda i, ps: (0, i))
    return pl.pallas_call(
        _slots_kernel,
        grid_spec=pltpu.PrefetchScalarGridSpec(
            num_scalar_prefetch=1, grid=(t // tm,), in_specs=[blk(), blk()], out_specs=blk()),
        out_shape=jax.ShapeDtypeStruct((k, t), I32),
        compiler_params=_params(("arbitrary",), 32),
    )(pstarts, idx, rank)


def _pad_bits():
    bit = EXPERT_ROWS // 2
    while bit >= 1:
        yield bit
        bit //= 2


def _row_tile(ref, first_sublane_row, rows=1):
    return ref.at[pl.ds(pl.multiple_of(first_sublane_row, ROW_SUBLANES), rows * ROW_SUBLANES), :]


def _dispatch_kernel(cnt_ref, ps_ref, pc_ref, dest_ref, h_ref, hb_ref, wg_ref, wu_ref, wd_ref,
                     xs_ref, shared_ref, zero_ref, sem, pad_sem):
    tm = h_ref.shape[0] // ROW_SUBLANES

    def copy(t, k):
        return pltpu.make_async_copy(_row_tile(h_ref, t * ROW_SUBLANES),
                                     _row_tile(xs_ref, dest_ref[t * TOP_K + k]), sem)

    def start(t, carry):
        for k in range(TOP_K):
            copy(t, k).start(priority=k % 2)
        return carry

    def wait(t, carry):
        for k in range(TOP_K):
            copy(t, k).wait()
        return carry

    part = tm // 2
    for p in range(2):
        lax.fori_loop(p * part, (p + 1) * part, start, 0)
        rows = slice(p * part, (p + 1) * part)
        x = hb_ref[rows, :]
        g = _dot(x, wg_ref[...])
        u = _dot(x, wu_ref[...])
        shared_ref[rows, :] = _dot(((g * _sigmoid(g)) * u).astype(BF16), wd_ref[...])

    @pl.when(pl.program_id(0) == 0)
    def _():
        zero_ref[...] = jnp.zeros_like(zero_ref)

        def pad_copy(first, rows):
            return pltpu.make_async_copy(_row_tile(zero_ref, 0, rows),
                                         _row_tile(xs_ref, first * ROW_SUBLANES, rows), pad_sem)

        def pads(e, fn):
            pad = pc_ref[e] - cnt_ref[e]
            lo = ps_ref[e] + cnt_ref[e]
            for bit in _pad_bits():
                @pl.when((pad & bit) != 0)
                def _():
                    fn(pad_copy(lo + (pad // (2 * bit)) * (2 * bit), bit))

        def start_pads(e, carry):
            pads(e, lambda cp: cp.start())
            return carry

        def wait_pads(e, carry):
            pads(e, lambda cp: cp.wait())
            return carry

        lax.fori_loop(0, cnt_ref.shape[0], start_pads, 0)
        lax.fori_loop(0, cnt_ref.shape[0], wait_pads, 0)

    lax.fori_loop(0, tm, wait, 0)


def _dispatch(counts, pstarts, pcounts, dest, h2p, h2, wg, wu, wd, n_rows, tm):
    t, d = h2.shape
    f = wg.shape[1]
    const = lambda shape: pl.BlockSpec(shape, lambda i, *_: (0, 0))
    return pl.pallas_call(
        _dispatch_kernel,
        grid_spec=pltpu.PrefetchScalarGridSpec(
            num_scalar_prefetch=3,
            grid=(t // tm,),
            in_specs=[pl.BlockSpec((tm * TOP_K,), lambda i, *_: (i,), memory_space=pltpu.SMEM),
                      pl.BlockSpec((tm * ROW_SUBLANES, 128), lambda i, *_: (i, 0)),
                      pl.BlockSpec((tm, d), lambda i, *_: (i, 0)),
                      const((d, f)), const((d, f)), const((f, d))],
            out_specs=[pl.BlockSpec(memory_space=pl.ANY), pl.BlockSpec((tm, d), lambda i, *_: (i, 0))],
            scratch_shapes=[pltpu.VMEM((EXPERT_ROWS // 2 * ROW_SUBLANES, 128), I32),
                            pltpu.SemaphoreType.DMA, pltpu.SemaphoreType.DMA],
        ),
        out_shape=[jax.ShapeDtypeStruct((n_rows * ROW_SUBLANES, 128), I32), jax.ShapeDtypeStruct((t, d), F32)],
        compiler_params=_params(("arbitrary",), 40),
    )(counts, pstarts, pcounts, dest, h2p, h2, wg, wu, wd)


def _expert_kernel(be_ref, rows_ref, nu_ref, first_ref, slot_ref, ahead_ref, warm_ref, x_ref, wg_hbm, wu_hbm, wd_hbm,
                   o_ref, wgf, wuf, wdf, wgb, wub, wdb, sem):
    nb = pl.program_id(0)
    half_f = wd_hbm.shape[1] // 2

    def fetch(e, s):
        lo, hi = pl.ds(0, half_f), pl.ds(half_f, half_f)
        return (pltpu.make_async_copy(wg_hbm.at[e], wgf.at[s], sem.at[s, 0]),
                pltpu.make_async_copy(wu_hbm.at[e], wuf.at[s], sem.at[s, 1]),
                pltpu.make_async_copy(wd_hbm.at[e, lo], wdf.at[s, lo], sem.at[s, 2]),
                pltpu.make_async_copy(wd_hbm.at[e, hi], wdf.at[s, hi], sem.at[s, 3]))

    def start(e, s):
        for i, cp in enumerate(fetch(e, s)):
            cp.start(priority=i % 2)

    @pl.when(nb == 0)
    def _():
        for i in range(WEIGHT_SLOTS - 1):
            @pl.when(warm_ref[i] >= 0)
            def _():
                start(warm_ref[i], i)

    @pl.when(first_ref[nb] == 1)
    def _():
        s = slot_ref[nb]
        for cp in fetch(be_ref[nb], s):
            cp.wait()

        @pl.when(ahead_ref[nb] >= 0)
        def _():
            start(ahead_ref[nb], (s + WEIGHT_SLOTS - 1) % WEIGHT_SLOTS)

    def mlp(rows, opening):
        if opening:
            s = slot_ref[nb]
            wg, wu, wd = wgf[s].astype(BF16), wuf[s].astype(BF16), wdf[s].astype(BF16)
            wgb[...], wub[...], wdb[...] = wg, wu, wd
        else:
            wg, wu, wd = wgb[...], wub[...], wdb[...]
        x = _unpack_halves(_load_row_tiles(x_ref, rows)).astype(BF16)
        g = _dot(x, wg)
        u = _dot(x, wu)
        a = (g * _sigmoid(g)) * u
        _store_row_tiles(o_ref, _pack_halves(_dot(a.astype(BF16), wd)))
        if rows < EXPERT_ROWS:
            o_ref[rows * ROW_SUBLANES:, :] = jnp.zeros(((EXPERT_ROWS - rows) * ROW_SUBLANES, 128), I32)

    half = EXPERT_ROWS // 2
    n_real = rows_ref[nb]
    for opening in (True, False):
        is_kind = (first_ref[nb] == 1) == opening
        pl.when(is_kind & (n_real > half))(functools.partial(mlp, EXPERT_ROWS, opening))
        pl.when(is_kind & (n_real > 0) & (n_real <= half))(functools.partial(mlp, half, opening))


def _experts(block_e, rows, n_used, first, slot, ahead_e, warm_e, xs, w_gate, w_up, w_down):
    r, dp = xs.shape
    ne, d, f = w_gate.shape
    bm = EXPERT_ROWS * ROW_SUBLANES
    ws = WEIGHT_SLOTS
    row_blk = lambda nb, be, rw, nu, *_: (jnp.minimum(nb, nu[0] - 1), 0)
    return pl.pallas_call(
        _expert_kernel,
        grid_spec=pltpu.PrefetchScalarGridSpec(
            num_scalar_prefetch=7,
            grid=(r // bm,),
            in_specs=[pl.BlockSpec((bm, dp), row_blk),
                      pl.BlockSpec(memory_space=pl.ANY),
                      pl.BlockSpec(memory_space=pl.ANY),
                      pl.BlockSpec(memory_space=pl.ANY)],
            out_specs=pl.BlockSpec((bm, dp), row_blk),
            scratch_shapes=[pltpu.VMEM((ws, d, f), F32), pltpu.VMEM((ws, d, f), F32), pltpu.VMEM((ws, f, d), F32),
                            pltpu.VMEM((d, f), BF16), pltpu.VMEM((d, f), BF16), pltpu.VMEM((f, d), BF16),
                            pltpu.SemaphoreType.DMA((ws, 4))],
        ),
        out_shape=jax.ShapeDtypeStruct((r, dp), I32),
        compiler_params=_params(("arbitrary",), 60),
    )(block_e, rows, n_used, first, slot, ahead_e, warm_e, xs, w_gate, w_up, w_down)


def _combine_kernel(dest_ref, dnext_ref, w_ref, x1_ref, shared_ref, gt_ref, fg_ref, ys_ref,
                    o_ref, buf0, buf1, routed_ref, sem):
    buf = (buf0, buf1)
    i = pl.program_id(0)
    last = pl.num_programs(0) - 1
    tm = x1_ref.shape[0]

    def copy(idx_ref, s, t, k):
        return pltpu.make_async_copy(_row_tile(ys_ref, idx_ref[t * TOP_K + k]),
                                     _row_tile(buf[s].at[k], t * ROW_SUBLANES), sem.at[s])

    def start_token(idx_ref, s, t):
        for k in range(TOP_K):
            copy(idx_ref, s, t, k).start(priority=k % 2)

    def wait_tile(s):
        def body(t, carry):
            for k in range(TOP_K):
                copy(dest_ref, s, t, k).wait()
            return carry
        lax.fori_loop(0, tm, body, 0)

    @pl.when(i == 0)
    def _():
        def body(t, carry):
            start_token(dest_ref, 0, t)
            return carry
        lax.fori_loop(0, tm, body, 0)

    def gather_reduce(slot):
        wait_tile(slot)

        def group(g, carry):
            for r in range(8):
                start_token(dnext_ref, 1 - slot, g * 8 + r)
            base = pl.multiple_of(g * (8 * ROW_SUBLANES), 8 * ROW_SUBLANES)
            w8 = w_ref[pl.ds(pl.multiple_of(g * 8, 8), 8), :]
            lo = [None] * ROW_SUBLANES
            hi = [None] * ROW_SUBLANES
            for k in range(TOP_K):
                wk = w8[:, k:k + 1]
                for s in range(ROW_SUBLANES):
                    p = buf[slot][k, pl.ds(base + s, 8, stride=ROW_SUBLANES), :]
                    l = lax.bitcast_convert_type(lax.shift_left(p, 16), F32) * wk
                    h = lax.bitcast_convert_type(p & -65536, F32) * wk
                    lo[s] = l if k == 0 else lo[s] + l
                    hi[s] = h if k == 0 else hi[s] + h
            routed_ref[pl.ds(pl.multiple_of(g * 8, 8), 8), :] = jnp.concatenate(lo + hi, axis=1)
            return carry

        lax.fori_loop(0, tm // 8, group, 0)

    for slot in range(2):
        pl.when(i % 2 == slot)(functools.partial(gather_reduce, slot))

    y = x1_ref[...] + gt_ref[0] * (routed_ref[...] + shared_ref[...])
    ms = jnp.mean(y * y, axis=-1, keepdims=True)
    o_ref[...] = y * lax.rsqrt(ms + RMS_EPS) * fg_ref[...]

    for slot in range(2):
        pl.when((i == last) & (i % 2 == slot))(functools.partial(wait_tile, 1 - slot))


def _combine(dest, w_tok, x1, shared, gt2, fg, ys, seq, tm):
    t, d = x1.shape
    nt = t // tm
    per_b = seq // tm
    const = lambda shape: pl.BlockSpec(shape, lambda i: (0, 0))
    return pl.pallas_call(
        _combine_kernel,
        grid=(nt,),
        in_specs=[pl.BlockSpec((tm * TOP_K,), lambda i: (i,), memory_space=pltpu.SMEM),
                  pl.BlockSpec((tm * TOP_K,), lambda i: (jnp.minimum(i + 1, nt - 1),), memory_space=pltpu.SMEM),
                  pl.BlockSpec((tm, TOP_K), lambda i: (i, 0)),
                  pl.BlockSpec((tm, d), lambda i: (i, 0)),
                  pl.BlockSpec((tm, d), lambda i: (i, 0)),
                  pl.BlockSpec((1, 1, d), lambda i: (i // per_b, 0, 0)),
                  const((1, d)),
                  pl.BlockSpec(memory_space=pl.ANY)],
        out_specs=pl.BlockSpec((tm, d), lambda i: (i, 0)),
        out_shape=jax.ShapeDtypeStruct((t, d), F32),
        scratch_shapes=[pltpu.VMEM((TOP_K, tm * ROW_SUBLANES, 128), I32),
                        pltpu.VMEM((TOP_K, tm * ROW_SUBLANES, 128), I32), pltpu.VMEM((tm, d), F32),
                        pltpu.SemaphoreType.DMA((2,))],
        compiler_params=_params(("arbitrary",), 56),
    )(dest, dest, w_tok, x1, shared, gt2, fg, ys)


def _block_tables(pends, row_ends, n_blocks):
    bm = EXPERT_ROWS
    n_used = (pends[-1] // bm).astype(I32)
    j = jnp.arange(n_blocks, dtype=I32)
    jc = jnp.minimum(j, n_used - 1)
    block_e = jnp.sum((pends[None, :] <= (jc * bm)[:, None]).astype(I32), axis=1)
    rows = jnp.where(j < n_used, jnp.clip(row_ends[block_e] - j * bm, 0, bm), 0).astype(I32)
    prev_e = jnp.concatenate([jnp.full((1,), -1, I32), block_e[:-1]])
    first = ((block_e != prev_e) & (j < n_used)).astype(I32)
    slot = (jnp.cumsum(first) - 1) % WEIGHT_SLOTS
    big = jnp.int32(n_blocks)
    later = lax.cummin(jnp.where(first == 1, j, big)[::-1])[::-1]
    nxt = jnp.concatenate([later[1:], big[None]])
    hop = lambda at: jnp.where(at < big, nxt[jnp.minimum(at, n_blocks - 1)], big)
    expert_at = lambda at: jnp.where(at < big, block_e[jnp.minimum(at, n_blocks - 1)], -1).astype(I32)
    ahead = j
    warm = []
    for _ in range(WEIGHT_SLOTS - 1):
        warm.append(expert_at(ahead[0]))
        ahead = hop(ahead)
    return block_e, rows, n_used[None], first, slot.astype(I32), expert_at(ahead), jnp.stack(warm)


def kernel(x, c, w_ada, b_ada, norm1_g, w_in, conv_w, conv_b, lru_wa, lru_ba, lru_wx, lru_bx,
           lru_lambda, hgrn_lb, hgrn_norm_g, w_out, norm2_g, w_router, router_bias,
           w_gate, w_up, w_down, ws_gate, ws_up, ws_down, final_g):
    batch, seq, d = x.shape
    t = batch * seq
    depth = w_ada.shape[0]
    d_lru = conv_w.shape[2]
    d_hgrn = d - d_lru
    heads = d_hgrn // HEAD_DIM
    ne = w_router.shape[2]
    assert d_lru % LRU_BLOCK == 0 and d_hgrn % HEAD_DIM == 0 and seq % 512 == 0 and ne % (8 * N_GROUPS) == 0
    assert depth == 1, "the final norm is fused into the combine kernel"
    assert d == 2 * 128 * ROW_SUBLANES, "a packed row must be exactly one (ROW_SUBLANES, 128) tile"
    l = 0

    lower_bounds = jnp.cumsum(jax.nn.softmax(hgrn_lb.astype(F32), axis=0), axis=0)
    c_pad = jnp.pad(c, ((0, -batch % 8), (0, 0)))
    mod = _adaln_mod(c_pad, w_ada[l], b_ada[l][None, :])[:batch]
    sh1, sc1, gt1, sh2, sc2, gt2 = [m[:, None, :] for m in jnp.split(mod, 6, axis=-1)]

    proj = _norm_matmul(x, norm1_g[l][None, :], sc1, sh1, w_in[l].astype(BF16), min(1024, seq), 2048, BF16)
    y_lru = _rg_lru(proj, conv_w[l], conv_b[l][None, :], lru_wa[l].astype(BF16), lru_ba[l][None, :],
                    lru_wx[l].astype(BF16), lru_bx[l][None, :], lru_lambda[l][None, :], batch, seq, 256)
    y_hgrn = _hgrn2(proj, lower_bounds[l][None, :], hgrn_norm_g[l][None, :], batch, seq, heads,
                    2 * d_lru // HEAD_DIM, min(seq, 2048), 4)
    wo = w_out[l].astype(BF16)
    x1, h2, h2p = _out_proj(y_lru, y_hgrn, wo[:d_lru], wo[d_lru:], x, gt1, norm2_g[l][None, :], sc2, sh2, 512)

    idx, w_sel, rank, counts = _route(h2, w_router[l].T.astype(BF16), router_bias[l][:, None], 512)
    counts = counts[:, 0]
    bm = EXPERT_ROWS
    pcounts = (counts + bm - 1) // bm * bm
    pends = jnp.cumsum(pcounts)
    pstarts = pends - pcounts
    n_rows = -(-(t * TOP_K + ne * (bm - 1)) // bm) * bm
    dest = _slots(pstarts, idx, rank, min(2048, t)).T.reshape(-1)
    xs, shared = _dispatch(counts, pstarts, pcounts, dest, h2p, h2, ws_gate[l].astype(BF16),
                           ws_up[l].astype(BF16), ws_down[l].astype(BF16), n_rows, 256)
    ys = _experts(*_block_tables(pends, pstarts + counts, n_rows // bm), xs, w_gate[l], w_up[l], w_down[l])
    out = _combine(dest, w_sel.T, x1, shared, gt2, final_g[None, :], ys, seq, 256)
    return out.reshape(batch, seq, d)
```

```python
import functools

import jax
import jax.numpy as jnp
from jax import lax
from jax.experimental import pallas as pl
from jax.experimental.pallas import tpu as pltpu

F32 = jnp.float32
BF16 = jnp.bfloat16
I32 = jnp.int32

RMS_EPS = 1e-6
LRU_C = 8.0
LRU_BLOCK = 128
CONV_WIDTH = 4
HEAD_DIM = 128
CHUNK = 64
SUB = 16
N_GROUPS = 8
TOPK_GROUPS = 4
TOP_K = 8
ROUTED_SCALE = 2.5
EXPERT_ROWS = 256
ROW_SUBLANES = 8
WEIGHT_SLOTS = 3
MIB = 1024 * 1024


def _params(sem, vmem_mib):
    return pltpu.CompilerParams(dimension_semantics=sem, vmem_limit_bytes=vmem_mib * MIB)


def _sigmoid(x):
    return 0.5 * jnp.tanh(0.5 * x) + 0.5


def _dot(a, b):
    return jnp.dot(a, b, preferred_element_type=F32)


def _dot_nt(a, b):
    return lax.dot_general(a, b, (((1,), (1,)), ((), ())), preferred_element_type=F32)


def _dot_tn(a, b):
    return lax.dot_general(a, b, (((0,), (0,)), ((), ())), preferred_element_type=F32)


def _mod_kernel(c_ref, w_ref, b_ref, o_ref):
    c = c_ref[...]
    cond = c * _sigmoid(c)
    o_ref[...] = _dot(cond.astype(BF16), w_ref[...].astype(BF16)) + b_ref[...]


def _adaln_mod(c_pad, w_ada, b_ada):
    rows, d = c_pad.shape
    n = w_ada.shape[1]
    bn = 1024
    return pl.pallas_call(
        _mod_kernel,
        grid=(n // bn,),
        in_specs=[pl.BlockSpec((rows, d), lambda j: (0, 0)),
                  pl.BlockSpec((d, bn), lambda j: (0, j)),
                  pl.BlockSpec((1, bn), lambda j: (0, j))],
        out_specs=pl.BlockSpec((rows, bn), lambda j: (0, j)),
        out_shape=jax.ShapeDtypeStruct((rows, n), F32),
        compiler_params=_params(("arbitrary",), 40),
    )(c_pad, w_ada, b_ada)


def _norm_mm_kernel(x_ref, g_ref, sc_ref, sh_ref, w_ref, o_ref):
    x = x_ref[0]
    ms = jnp.mean(x * x, axis=-1, keepdims=True)
    y = x * lax.rsqrt(ms + RMS_EPS) * g_ref[...]
    h = (y * (1.0 + sc_ref[0]) + sh_ref[0]).astype(BF16)
    o_ref[...] = _dot(h, w_ref[...]).astype(o_ref.dtype)


def _norm_matmul(x, g, sc, sh, w, bm, bn, out_dtype):
    b, l, k = x.shape
    n = w.shape[1]
    nt = l // bm
    bvec = lambda: pl.BlockSpec((1, 1, k), lambda j, i, t: (i, 0, 0))
    return pl.pallas_call(
        _norm_mm_kernel,
        grid=(n // bn, b, nt),
        in_specs=[pl.BlockSpec((1, bm, k), lambda j, i, t: (i, t, 0)),
                  pl.BlockSpec((1, k), lambda j, i, t: (0, 0)),
                  bvec(), bvec(),
                  pl.BlockSpec((k, bn), lambda j, i, t: (0, j))],
        out_specs=pl.BlockSpec((bm, bn), lambda j, i, t: (i * nt + t, j)),
        out_shape=jax.ShapeDtypeStruct((b * l, n), out_dtype),
        compiler_params=_params(("arbitrary", "arbitrary", "arbitrary"), 56),
    )(x, g, sc, sh, w)


def _shift_rows(u, prev, k):
    if k == 0:
        return u
    tt, c = u.shape
    rot = pltpu.roll(u.reshape(tt // 8, 8, c), k, 1)
    before = jnp.concatenate([pltpu.roll(prev, k, 0)[None], rot[:-1]], axis=0)
    r = lax.broadcasted_iota(I32, rot.shape, 1)
    return jnp.where(r < k, before, rot).reshape(tt, c)


def _lru_kernel(u_ref, z_ref, cw_ref, cb_ref, wa_ref, ba_ref, wx_ref, bx_ref, lam_ref,
                o_ref, prev_ref, h_ref):
    j = pl.program_id(1)
    tt, c = u_ref.shape

    @pl.when(j == 0)
    def _():
        prev_ref[...] = jnp.zeros_like(prev_ref)
        h_ref[...] = jnp.zeros_like(h_ref)

    u = u_ref[...].astype(F32)
    prev = prev_ref[...]
    xc = jnp.broadcast_to(cb_ref[...], (tt, c))
    for w in range(CONV_WIDTH):
        xc = xc + cw_ref[w:w + 1, :] * _shift_rows(u, prev, CONV_WIDTH - 1 - w)
    prev_ref[...] = u[tt - 8:, :]

    xcb = xc.astype(BF16)
    ra, rx = [], []
    for blk in range(c // LRU_BLOCK):
        xs = xcb[:, blk * LRU_BLOCK:(blk + 1) * LRU_BLOCK]
        ra.append(_dot(xs, wa_ref[blk]))
        rx.append(_dot(xs, wx_ref[blk]))
    r = _sigmoid(jnp.concatenate(ra, axis=1) + ba_ref[...])
    ig = _sigmoid(jnp.concatenate(rx, axis=1) + bx_ref[...])

    nl = -lam_ref[...]
    softplus = jnp.maximum(nl, 0.0) + jnp.log1p(jnp.exp(-jnp.abs(nl)))
    log_a = (-LRU_C) * r * softplus
    a = jnp.exp(log_a)
    mult = jnp.sqrt(1.0 - jnp.exp(2.0 * log_a))
    row = lax.broadcasted_iota(I32, (tt, c), 0)
    mult = jnp.where((row == 0) & (j == 0), 1.0, mult)
    bv = mult * ig * xc

    a3 = a.reshape(tt // 8, 8, c)
    b3 = bv.reshape(tt // 8, 8, c)
    in_group = lax.broadcasted_iota(I32, a3.shape, 1)
    s = 1
    while s < 8:
        keep = in_group >= s
        b3 = jnp.where(keep, a3 * pltpu.roll(b3, s, 1) + b3, b3)
        a3 = jnp.where(keep, a3 * pltpu.roll(a3, s, 1), a3)
        s *= 2
    carry = h_ref[...]
    groups = []
    for gi in range(tt // 8):
        hg = b3[gi] + a3[gi] * carry
        carry = hg[7:8, :]
        groups.append(hg)
    h = jnp.concatenate(groups, axis=0)
    h_ref[...] = carry

    z = z_ref[...].astype(F32)
    gelu = 0.5 * z * (1.0 + jnp.tanh(0.7978845608028654 * (z + 0.044715 * (z * z * z))))
    o_ref[...] = (h * gelu).astype(o_ref.dtype)


def _rg_lru(proj, conv_w, conv_b, wa, ba, wx, bx, lam, batch, seq, tt):
    t = proj.shape[0]
    c = conv_w.shape[1]
    nt = seq // tt
    vec = lambda: pl.BlockSpec((1, c), lambda i, j: (0, 0))
    mat = lambda: pl.BlockSpec(wa.shape, lambda i, j: (0, 0, 0))
    return pl.pallas_call(
        _lru_kernel,
        grid=(batch, nt),
        in_specs=[pl.BlockSpec((tt, c), lambda i, j: (i * nt + j, 0)),
                  pl.BlockSpec((tt, c), lambda i, j: (i * nt + j, 1)),
                  pl.BlockSpec((CONV_WIDTH, c), lambda i, j: (0, 0)),
                  vec(), mat(), vec(), mat(), vec(), vec()],
        out_specs=pl.BlockSpec((tt, c), lambda i, j: (i * nt + j, 0)),
        out_shape=jax.ShapeDtypeStruct((t, c), BF16),
        scratch_shapes=[pltpu.VMEM((8, c), F32), pltpu.VMEM((1, c), F32)],
        compiler_params=_params(("arbitrary", "arbitrary"), 48),
    )(proj, proj, conv_w, conv_b, wa, ba, wx, bx, lam)


def _hgrn_lanes(wid):
    return [slice(h * HEAD_DIM, (h + 1) * HEAD_DIM) for h in range(wid // HEAD_DIM)]


def _per_head(wid, fn):
    return jnp.concatenate([fn(h, ln) for h, ln in enumerate(_hgrn_lanes(wid))], axis=1)


def _row_on_sublanes(slab_ref, r):
    return jnp.concatenate([slab_ref[h, pl.ds(r, 8, stride=0), :] for h in range(slab_ref.shape[0])], axis=1)


_N_SUB = CHUNK // SUB
_STACK_ROWS = 8 * 16 + 8 * 8
_KT_OFFSET = [SUB * i * (i - 1) // 2 for i in range(_N_SUB + 1)]


def _hgrn_front_shapes(wid):
    return [pltpu.VMEM((CHUNK, wid), BF16),
            pltpu.VMEM((CHUNK, wid), BF16),
            pltpu.VMEM((CHUNK, wid), BF16),
            pltpu.VMEM((CHUNK, wid), BF16),
            pltpu.VMEM((1, wid), F32),
            pltpu.VMEM((CHUNK, wid), F32),
            pltpu.VMEM((wid // HEAD_DIM, CHUNK, HEAD_DIM), F32),
            pltpu.VMEM((wid // HEAD_DIM, CHUNK, HEAD_DIM), F32),
            pltpu.VMEM((_KT_OFFSET[_N_SUB], wid), BF16),
            pltpu.VMEM((_N_SUB * _STACK_ROWS, wid), BF16)]


def _hgrn_front(qi, fi, v, gi, lb, out):
    qe_ref, qt_ref, vb_ref, kd_ref, decay_ref, gate_ref, v_ref, c_ref, kt_ref, stack_ref = out
    n, wid = qi.shape
    q = qi * _sigmoid(qi)
    f = lb + (1.0 - lb) * _sigmoid(fi)
    row = lax.broadcasted_iota(I32, (n, wid), 0)
    b = jnp.log2(f)
    s = 1
    while s < n:
        b = b + jnp.where(row >= s, pltpu.roll(b, s, 0), 0.0)
        s *= 2
    c = b - jnp.log2(1.0 - f)
    qe_ref[...] = (q * jnp.exp2(b)).astype(BF16)
    refb = jnp.concatenate(
        [jnp.broadcast_to(b[i * SUB:i * SUB + 1, :], (SUB, wid)) for i in range(_N_SUB)], axis=0)
    qt_ref[...] = (q * jnp.exp2(b - refb)).astype(BF16)
    vb_ref[...] = v.astype(BF16)
    for h, ln in enumerate(_hgrn_lanes(wid)):
        v_ref[h] = v[:, ln]
        c_ref[h] = c[:, ln]
    gate_ref[...] = gi * _sigmoid(gi)
    b_last = b[n - 1:n, :]
    kd_ref[...] = jnp.exp2(b_last - c).astype(BF16)
    decay_ref[...] = jnp.exp2(b_last)
    row8 = lax.broadcasted_iota(I32, (8, wid), 0)
    for i in range(_N_SUB):
        lo = i * SUB
        if i > 0:
            kt_ref[_KT_OFFSET[i]:_KT_OFFSET[i + 1], :] = jnp.exp2(b[lo:lo + 1, :] - c[:lo, :]).astype(BF16)
        q0, q1 = q[lo:lo + 8, :], q[lo + 8:lo + SUB, :]
        b0, b1 = b[lo:lo + 8, :], b[lo + 8:lo + SUB, :]
        base = i * _STACK_ROWS
        pending = None
        for s_ in range(SUB):
            cs = _row_on_sublanes(c_ref, lo + s_)
            p1 = q1 * jnp.exp2(b1 - cs)
            if s_ < 8:
                pair = [jnp.where(row8 >= s_, q0 * jnp.exp2(b0 - cs), 0.0), p1]
                at = base + 16 * s_
            elif pending is None:
                pending = jnp.where(row8 >= s_ - 8, p1, 0.0)
                continue
            else:
                pair = [pending, jnp.where(row8 >= s_ - 8, p1, 0.0)]
                pending = None
                at = base + 64 + 8 * (s_ - 1)
            stack_ref[at:at + 16, :] = jnp.concatenate(pair, axis=0).astype(BF16)


def _hgrn_products(front, states, ones_b):
    qe_ref, qt_ref, _, _, _, _, _, _, kt_ref, stack_ref = front
    wid = qe_ref.shape[1]
    lanes = _hgrn_lanes(wid)
    o_state = _per_head(wid, lambda h, ln: _dot_nt(qe_ref[:, ln], states[h].astype(BF16)))
    scores = [[_dot_nt(qt_ref[i * SUB:(i + 1) * SUB, ln], kt_ref[_KT_OFFSET[i]:_KT_OFFSET[i + 1], ln]).astype(BF16)
               for ln in lanes] for i in range(1, _N_SUB)]
    reds = [_per_head(wid, lambda h, ln: _dot(stack_ref[i * _STACK_ROWS:(i + 1) * _STACK_ROWS, ln], ones_b))
            for i in range(_N_SUB)]
    return o_state, scores, reds


def _hgrn_finish(front, products, states, ng):
    _, _, vb_ref, kd_ref, decay_ref, gate_ref, v_ref, _, _, _ = front
    o_state, scores, reds = products
    n, wid = vb_ref.shape
    outs = []
    for i in range(_N_SUB):
        lo = i * SUB
        acc = o_state[lo:lo + SUB, :]
        if i > 0:
            sc = scores[i - 1]
            acc = acc + _per_head(wid, lambda h, ln: _dot(sc[h], vb_ref[:lo, ln]))
        red = reds[i]
        acc0 = jnp.zeros((8, wid), F32)
        acc1 = jnp.zeros((8, wid), F32)
        for s_ in range(SUB):
            vs = _row_on_sublanes(v_ref, lo + s_)
            if s_ < 8:
                acc0 = acc0 + red[16 * s_:16 * s_ + 8, :] * vs
                acc1 = acc1 + red[16 * s_ + 8:16 * s_ + 16, :] * vs
            else:
                acc1 = acc1 + red[64 + 8 * s_:72 + 8 * s_, :] * vs
        outs.append(acc + jnp.concatenate([acc0, acc1], axis=0))
    o = jnp.concatenate(outs, axis=0)
    decay = decay_ref[...]
    new_states = [states[h] * decay[:, ln] + _dot_tn(vb_ref[:, ln], kd_ref[:, ln])
                  for h, ln in enumerate(_hgrn_lanes(wid))]
    oo = o * o
    scale = _per_head(wid, lambda h, ln: jnp.broadcast_to(
        lax.rsqrt(jnp.mean(oo[:, ln], axis=-1, keepdims=True) + RMS_EPS), (n, HEAD_DIM)) * ng)
    return o * scale * gate_ref[...], new_states


def _hgrn_kernel(q_ref, f_ref, v_ref, g_ref, lb_ref, ng_ref, o_ref, st_ref, *front_refs):
    @pl.when(pl.program_id(2) == 0)
    def _():
        st_ref[...] = jnp.zeros_like(st_ref)

    lb = lb_ref[...]
    ng = ng_ref[...]
    heads = q_ref.shape[1] // HEAD_DIM
    n_chunks = q_ref.shape[0] // CHUNK
    ones_b = jnp.ones((HEAD_DIM, HEAD_DIM), BF16)
    half = len(front_refs) // 2
    fronts = (front_refs[:half], front_refs[half:])

    def chunk_rows(ci):
        return pl.ds(pl.multiple_of(ci * CHUNK, CHUNK), CHUNK)

    def front(ci, out):
        rows = chunk_rows(ci)
        _hgrn_front(q_ref[rows, :].astype(F32), f_ref[rows, :].astype(F32),
                    v_ref[rows, :].astype(F32), g_ref[rows, :].astype(F32), lb, out)

    def back(ci, cur, nxt):
        states = [st_ref[h] for h in range(heads)]
        products = _hgrn_products(cur, states, ones_b)
        front(jnp.minimum(ci + 1, n_chunks - 1), nxt)
        y, new_states = _hgrn_finish(cur, products, states, ng)
        for h in range(heads):
            st_ref[h] = new_states[h]
        o_ref[chunk_rows(ci), :] = y.astype(o_ref.dtype)

    def body(pair, carry):
        back(2 * pair, fronts[0], fronts[1])
        back(2 * pair + 1, fronts[1], fronts[0])
        return carry

    front(0, fronts[0])
    lax.fori_loop(0, n_chunks // 2, body, 0)


def _hgrn2(proj, lb, norm_g, batch, seq, heads, col0, tt, hp):
    t = proj.shape[0]
    nt = seq // tt
    wid = hp * HEAD_DIM
    assert heads % hp == 0 and col0 % hp == 0
    col = lambda off: pl.BlockSpec((tt, wid), lambda i, h, j: (i * nt + j, (col0 + off * heads) // hp + h))
    return pl.pallas_call(
        _hgrn_kernel,
        grid=(batch, heads // hp, nt),
        in_specs=[col(0), col(1), col(2), col(3),
                  pl.BlockSpec((1, wid), lambda i, h, j: (0, h)),
                  pl.BlockSpec((1, HEAD_DIM), lambda i, h, j: (0, 0))],
        out_specs=pl.BlockSpec((tt, wid), lambda i, h, j: (i * nt + j, h)),
        out_shape=jax.ShapeDtypeStruct((t, heads * HEAD_DIM), BF16),
        scratch_shapes=[pltpu.VMEM((hp, HEAD_DIM, HEAD_DIM), F32)] + 2 * _hgrn_front_shapes(wid),
        compiler_params=_params(("arbitrary", "arbitrary", "arbitrary"), 48),
    )(proj, proj, proj, proj, lb, norm_g)


def _pack_halves(y):
    n = y.shape[1] // 2
    lo = lax.bitcast_convert_type(y[:, :n].astype(BF16).astype(F32), I32)
    hi = lax.bitcast_convert_type(y[:, n:].astype(BF16).astype(F32), I32)
    return (hi & -65536) | lax.shift_right_logical(lo, 16)


def _unpack_halves(p):
    lo = lax.bitcast_convert_type(lax.shift_left(p, 16), F32)
    hi = lax.bitcast_convert_type(p & -65536, F32)
    return jnp.concatenate([lo, hi], axis=1)


def _store_row_tiles(ref, packed):
    m = packed.shape[0]
    for s in range(ROW_SUBLANES):
        ref[pl.ds(s, m, stride=ROW_SUBLANES), :] = packed[:, s * 128:(s + 1) * 128]


def _load_row_tiles(ref, m):
    return jnp.concatenate([ref[pl.ds(s, m, stride=ROW_SUBLANES), :] for s in range(ROW_SUBLANES)], axis=1)


def _out_kernel(yl_ref, yh_ref, wl_ref, wh_ref, x_ref, gt_ref, g2_ref, sc_ref, sh_ref, x1_ref, h2_ref, h2p_ref):
    mix = _dot(yl_ref[...], wl_ref[...]) + _dot(yh_ref[...], wh_ref[...])
    x1 = x_ref[0] + gt_ref[0] * mix
    x1_ref[...] = x1
    ms = jnp.mean(x1 * x1, axis=-1, keepdims=True)
    y = x1 * lax.rsqrt(ms + RMS_EPS) * g2_ref[...]
    h2 = y * (1.0 + sc_ref[0]) + sh_ref[0]
    h2_ref[...] = h2.astype(BF16)
    _store_row_tiles(h2p_ref, _pack_halves(h2))


def _out_proj(yl, yh, wl, wh, x, gt1, g2, sc2, sh2, bm):
    b, l, d = x.shape
    t = b * l
    nt = l // bm
    cl, ch = yl.shape[1], yh.shape[1]
    bvec = lambda: pl.BlockSpec((1, 1, d), lambda i, j: (i, 0, 0))
    return pl.pallas_call(
        _out_kernel,
        grid=(b, nt),
        in_specs=[pl.BlockSpec((bm, cl), lambda i, j: (i * nt + j, 0)),
                  pl.BlockSpec((bm, ch), lambda i, j: (i * nt + j, 0)),
                  pl.BlockSpec((cl, d), lambda i, j: (0, 0)),
                  pl.BlockSpec((ch, d), lambda i, j: (0, 0)),
                  pl.BlockSpec((1, bm, d), lambda i, j: (i, j, 0)),
                  bvec(),
                  pl.BlockSpec((1, d), lambda i, j: (0, 0)),
                  bvec(), bvec()],
        out_specs=[pl.BlockSpec((bm, d), lambda i, j: (i * nt + j, 0)),
                   pl.BlockSpec((bm, d), lambda i, j: (i * nt + j, 0)),
                   pl.BlockSpec((bm * ROW_SUBLANES, 128), lambda i, j: (i * nt + j, 0))],
        out_shape=[jax.ShapeDtypeStruct((t, d), F32), jax.ShapeDtypeStruct((t, d), BF16),
                   jax.ShapeDtypeStruct((t * ROW_SUBLANES, 128), I32)],
        compiler_params=_params(("arbitrary", "arbitrary"), 56),
    )(yl, yh, wl, wh, x, gt1, g2, sc2, sh2)


def _route_kernel(h_ref, wr_ref, bias_ref, idx_ref, w_ref, rank_ref, cnt_ref, carry_ref):
    step = pl.program_id(0)
    tm = h_ref.shape[0]
    ne = wr_ref.shape[0]
    gsz = ne // N_GROUPS
    neg = -jnp.inf

    @pl.when(step == 0)
    def _():
        carry_ref[...] = jnp.zeros_like(carry_ref)

    logits = _dot_nt(wr_ref[...], h_ref[...])
    scores = _sigmoid(logits)
    sel = scores + bias_ref[...]

    sel3 = sel.reshape(N_GROUPS, gsz, tm)
    pos3 = lax.broadcasted_iota(I32, (N_GROUPS, gsz, tm), 1)
    m1 = jnp.max(sel3, axis=1, keepdims=True)
    i1 = jnp.min(jnp.where(sel3 == m1, pos3, gsz), axis=1, keepdims=True)
    m2 = jnp.max(jnp.where(pos3 == i1, neg, sel3), axis=1, keepdims=True)
    gs = (m1 + m2).reshape(N_GROUPS, tm)

    gidx = lax.broadcasted_iota(I32, (N_GROUPS, tm), 0)
    beaten = jnp.zeros((N_GROUPS, tm), I32)
    for gp in range(N_GROUPS):
        other = gs[gp:gp + 1, :]
        beats = (other > gs) | ((other == gs) & (gp < gidx))
        beaten = beaten + beats.astype(I32)
    gkeep = (beaten < TOPK_GROUPS).reshape(N_GROUPS, 1, tm)
    cur = jnp.where(gkeep, sel3, neg).reshape(ne, tm)

    eidx = lax.broadcasted_iota(I32, (ne, tm), 0)
    picked = jnp.zeros((ne, tm), jnp.bool_)
    idx_rows, w_rows = [], []
    for _ in range(TOP_K):
        m = jnp.max(cur, axis=0, keepdims=True)
        ik = jnp.min(jnp.where(cur == m, eidx, ne), axis=0, keepdims=True)
        hit = eidx == ik
        w_rows.append(jnp.sum(jnp.where(hit, scores, 0.0), axis=0, keepdims=True))
        idx_rows.append(ik)
        cur = jnp.where(hit, neg, cur)
        picked = picked | hit
    w = jnp.concatenate(w_rows, axis=0)
    idx_ref[...] = jnp.concatenate(idx_rows, axis=0)
    w_ref[...] = w / jnp.sum(w, axis=0, keepdims=True) * ROUTED_SCALE

    pf = picked.astype(F32)
    ta = lax.broadcasted_iota(I32, (tm, tm), 0)
    tb = lax.broadcasted_iota(I32, (tm, tm), 1)
    before = (ta < tb).astype(BF16)
    cnt = _dot(pf.astype(BF16), before) + carry_ref[...]
    rank_rows = [jnp.sum(jnp.where(eidx == ik, cnt, 0.0), axis=0, keepdims=True) for ik in idx_rows]
    rank_ref[...] = jnp.concatenate(rank_rows, axis=0).astype(I32)
    total = carry_ref[...] + jnp.sum(pf, axis=1, keepdims=True)
    carry_ref[...] = total
    cnt_ref[...] = jnp.broadcast_to(total, cnt_ref.shape).astype(I32)


def _route(h2, wr_t, bias_col, tm):
    t, d = h2.shape
    ne = wr_t.shape[0]
    row = lambda: pl.BlockSpec((TOP_K, tm), lambda i: (0, i))
    return pl.pallas_call(
        _route_kernel,
        grid=(t // tm,),
        in_specs=[pl.BlockSpec((tm, d), lambda i: (i, 0)),
                  pl.BlockSpec((ne, d), lambda i: (0, 0)),
                  pl.BlockSpec((ne, 1), lambda i: (0, 0))],
        out_specs=[row(), row(), row(), pl.BlockSpec((ne, 128), lambda i: (0, 0))],
        out_shape=[jax.ShapeDtypeStruct((TOP_K, t), I32), jax.ShapeDtypeStruct((TOP_K, t), F32),
                   jax.ShapeDtypeStruct((TOP_K, t), I32), jax.ShapeDtypeStruct((ne, 128), I32)],
        scratch_shapes=[pltpu.VMEM((ne, 1), F32)],
        compiler_params=_params(("arbitrary",), 40),
    )(h2, wr_t, bias_col)


def _slots_kernel(ps_ref, idx_ref, rank_ref, o_ref):
    idx = idx_ref[...]

    def body(e, acc):
        return jnp.where(idx == e, ps_ref[e], acc)

    row = lax.fori_loop(0, ps_ref.shape[0], body, jnp.zeros(idx.shape, I32)) + rank_ref[...]
    o_ref[...] = row * ROW_SUBLANES


def _slots(pstarts, idx, rank, tm):
    k, t = idx.shape
    blk = lambda: pl.BlockSpec((k, tm), lambda i, ps: (0, i))
    return pl.pallas_call(
        _slots_kernel,
        grid_spec=pltpu.PrefetchScalarGridSpec(
            num_scalar_prefetch=1, grid=(t // tm,), in_specs=[blk(), blk()], out_specs=blk()),
        out_shape=jax.ShapeDtypeStruct((k, t), I32),
        compiler_params=_params(("arbitrary",), 32),
    )(pstarts, idx, rank)


def _pad_bits():
    bit = EXPERT_ROWS // 2
    while bit >= 1:
        yield bit
        bit //= 2


def _row_tile(ref, first_sublane_row, rows=1):
    return ref.at[pl.ds(pl.multiple_of(first_sublane_row, ROW_SUBLANES), rows * ROW_SUBLANES), :]


def _dispatch_kernel(cnt_ref, ps_ref, pc_ref, dest_ref, h_ref, xs_ref, zero_ref, sem, pad_sem):
    tm = h_ref.shape[0] // ROW_SUBLANES

    def copy(t, k):
        return pltpu.make_async_copy(_row_tile(h_ref, t * ROW_SUBLANES),
                                     _row_tile(xs_ref, dest_ref[t * TOP_K + k]), sem)

    def start(t, carry):
        for k in range(TOP_K):
            copy(t, k).start(priority=k % 2)
        return carry

    def wait_all():
        for _ in range(TOP_K):
            pltpu.make_async_copy(h_ref, xs_ref.at[pl.ds(0, tm * ROW_SUBLANES), :], sem).wait()

    lax.fori_loop(0, tm, start, 0)

    @pl.when(pl.program_id(0) == 0)
    def _():
        zero_ref[...] = jnp.zeros_like(zero_ref)

        def pad_copy(first, rows):
            return pltpu.make_async_copy(_row_tile(zero_ref, 0, rows),
                                         _row_tile(xs_ref, first * ROW_SUBLANES, rows), pad_sem)

        def pads(e, fn):
            pad = pc_ref[e] - cnt_ref[e]
            lo = ps_ref[e] + cnt_ref[e]
            for bit in _pad_bits():
                @pl.when((pad & bit) != 0)
                def _():
                    fn(pad_copy(lo + (pad // (2 * bit)) * (2 * bit), bit))

        def start_pads(e, carry):
            pads(e, lambda cp: cp.start())
            return carry

        def wait_pads(e, carry):
            pads(e, lambda cp: cp.wait())
            return carry

        lax.fori_loop(0, cnt_ref.shape[0], start_pads, 0)
        lax.fori_loop(0, cnt_ref.shape[0], wait_pads, 0)

    wait_all()


def _dispatch(counts, pstarts, pcounts, dest, h2p, n_rows, tm):
    t = h2p.shape[0] // ROW_SUBLANES
    return pl.pallas_call(
        _dispatch_kernel,
        grid_spec=pltpu.PrefetchScalarGridSpec(
            num_scalar_prefetch=3,
            grid=(t // tm,),
            in_specs=[pl.BlockSpec((tm * TOP_K,), lambda i, *_: (i,), memory_space=pltpu.SMEM),
                      pl.BlockSpec((tm * ROW_SUBLANES, 128), lambda i, *_: (i, 0))],
            out_specs=pl.BlockSpec(memory_space=pl.ANY),
            scratch_shapes=[pltpu.VMEM((EXPERT_ROWS // 2 * ROW_SUBLANES, 128), I32),
                            pltpu.SemaphoreType.DMA, pltpu.SemaphoreType.DMA],
        ),
        out_shape=jax.ShapeDtypeStruct((n_rows * ROW_SUBLANES, 128), I32),
        compiler_params=_params(("arbitrary",), 32),
    )(counts, pstarts, pcounts, dest, h2p)


def _expert_kernel(be_ref, rows_ref, nu_ref, first_ref, slot_ref, ahead_ref, warm_ref, x_ref, wg_hbm, wu_hbm, wd_hbm,
                   o_ref, wgf, wuf, wdf, wgb, wub, wdb, sem):
    nb = pl.program_id(0)
    half_f = wd_hbm.shape[1] // 2

    def fetch(e, s):
        lo, hi = pl.ds(0, half_f), pl.ds(half_f, half_f)
        return (pltpu.make_async_copy(wg_hbm.at[e], wgf.at[s], sem.at[s, 0]),
                pltpu.make_async_copy(wu_hbm.at[e], wuf.at[s], sem.at[s, 1]),
                pltpu.make_async_copy(wd_hbm.at[e, lo], wdf.at[s, lo], sem.at[s, 2]),
                pltpu.make_async_copy(wd_hbm.at[e, hi], wdf.at[s, hi], sem.at[s, 3]))

    def start(e, s):
        for i, cp in enumerate(fetch(e, s)):
            cp.start(priority=i % 2)

    @pl.when(nb == 0)
    def _():
        for i in range(WEIGHT_SLOTS - 1):
            @pl.when(warm_ref[i] >= 0)
            def _():
                start(warm_ref[i], i)

    @pl.when(first_ref[nb] == 1)
    def _():
        s = slot_ref[nb]
        for cp in fetch(be_ref[nb], s):
            cp.wait()

        @pl.when(ahead_ref[nb] >= 0)
        def _():
            start(ahead_ref[nb], (s + WEIGHT_SLOTS - 1) % WEIGHT_SLOTS)

    def mlp(rows, opening):
        if opening:
            s = slot_ref[nb]
            wg, wu, wd = wgf[s].astype(BF16), wuf[s].astype(BF16), wdf[s].astype(BF16)
            wgb[...], wub[...], wdb[...] = wg, wu, wd
        else:
            wg, wu, wd = wgb[...], wub[...], wdb[...]
        x = _unpack_halves(_load_row_tiles(x_ref, rows)).astype(BF16)
        g = _dot(x, wg)
        u = _dot(x, wu)
        a = (g * _sigmoid(g)) * u
        _store_row_tiles(o_ref, _pack_halves(_dot(a.astype(BF16), wd)))
        if rows < EXPERT_ROWS:
            o_ref[rows * ROW_SUBLANES:, :] = jnp.zeros(((EXPERT_ROWS - rows) * ROW_SUBLANES, 128), I32)

    half = EXPERT_ROWS // 2
    n_real = rows_ref[nb]
    for opening in (True, False):
        is_kind = (first_ref[nb] == 1) == opening
        pl.when(is_kind & (n_real > half))(functools.partial(mlp, EXPERT_ROWS, opening))
        pl.when(is_kind & (n_real > 0) & (n_real <= half))(functools.partial(mlp, half, opening))


def _experts(block_e, rows, n_used, first, slot, ahead_e, warm_e, xs, w_gate, w_up, w_down):
    r, dp = xs.shape
    ne, d, f = w_gate.shape
    bm = EXPERT_ROWS * ROW_SUBLANES
    ws = WEIGHT_SLOTS
    row_blk = lambda nb, be, rw, nu, *_: (jnp.minimum(nb, nu[0] - 1), 0)
    return pl.pallas_call(
        _expert_kernel,
        grid_spec=pltpu.PrefetchScalarGridSpec(
            num_scalar_prefetch=7,
            grid=(r // bm,),
            in_specs=[pl.BlockSpec((bm, dp), row_blk),
                      pl.BlockSpec(memory_space=pl.ANY),
                      pl.BlockSpec(memory_space=pl.ANY),
                      pl.BlockSpec(memory_space=pl.ANY)],
            out_specs=pl.BlockSpec((bm, dp), row_blk),
            scratch_shapes=[pltpu.VMEM((ws, d, f), F32), pltpu.VMEM((ws, d, f), F32), pltpu.VMEM((ws, f, d), F32),
                            pltpu.VMEM((d, f), BF16), pltpu.VMEM((d, f), BF16), pltpu.VMEM((f, d), BF16),
                            pltpu.SemaphoreType.DMA((ws, 4))],
        ),
        out_shape=jax.ShapeDtypeStruct((r, dp), I32),
        compiler_params=_params(("arbitrary",), 60),
    )(block_e, rows, n_used, first, slot, ahead_e, warm_e, xs, w_gate, w_up, w_down)


def _combine_kernel(dest_ref, dnext_ref, w_ref, x1_ref, h_ref, gt_ref, fg_ref, wg_ref, wu_ref, wd_ref, ys_ref,
                    o_ref, buf0, buf1, routed_ref, sem):
    buf = (buf0, buf1)
    i = pl.program_id(0)
    last = pl.num_programs(0) - 1
    tm = x1_ref.shape[0]

    def copy(idx_ref, s, t, k):
        return pltpu.make_async_copy(_row_tile(ys_ref, idx_ref[t * TOP_K + k]),
                                     _row_tile(buf[s].at[k], t * ROW_SUBLANES), sem.at[s])

    def start_token(idx_ref, s, t):
        for k in range(TOP_K):
            copy(idx_ref, s, t, k).start(priority=k % 2)

    def wait_tile(s):
        for k in range(TOP_K):
            pltpu.make_async_copy(ys_ref.at[pl.ds(0, tm * ROW_SUBLANES), :], buf[s].at[k], sem.at[s]).wait()

    @pl.when(i == 0)
    def _():
        def body(t, carry):
            start_token(dest_ref, 0, t)
            return carry
        lax.fori_loop(0, tm, body, 0)

    def gather_reduce(slot):
        wait_tile(slot)

        def group(g, carry):
            for r in range(8):
                start_token(dnext_ref, 1 - slot, g * 8 + r)
            base = pl.multiple_of(g * (8 * ROW_SUBLANES), 8 * ROW_SUBLANES)
            w8 = w_ref[pl.ds(pl.multiple_of(g * 8, 8), 8), :]
            lo = [None] * ROW_SUBLANES
            hi = [None] * ROW_SUBLANES
            for k in range(TOP_K):
                wk = w8[:, k:k + 1]
                for s in range(ROW_SUBLANES):
                    p = buf[slot][k, pl.ds(base + s, 8, stride=ROW_SUBLANES), :]
                    l = lax.bitcast_convert_type(lax.shift_left(p, 16), F32) * wk
                    h = lax.bitcast_convert_type(p & -65536, F32) * wk
                    lo[s] = l if k == 0 else lo[s] + l
                    hi[s] = h if k == 0 else hi[s] + h
            routed_ref[pl.ds(pl.multiple_of(g * 8, 8), 8), :] = jnp.concatenate(lo + hi, axis=1)
            return carry

        lax.fori_loop(0, tm // 8, group, 0)

    for slot in range(2):
        pl.when(i % 2 == slot)(functools.partial(gather_reduce, slot))

    x = h_ref[...]
    g = _dot(x, wg_ref[...])
    u = _dot(x, wu_ref[...])
    acc = _dot(((g * _sigmoid(g)) * u).astype(BF16), wd_ref[...])
    y = x1_ref[...] + gt_ref[0] * (routed_ref[...] + acc)
    ms = jnp.mean(y * y, axis=-1, keepdims=True)
    o_ref[...] = y * lax.rsqrt(ms + RMS_EPS) * fg_ref[...]

    for slot in range(2):
        pl.when((i == last) & (i % 2 == slot))(functools.partial(wait_tile, 1 - slot))


def _combine(dest, w_tok, x1, h2, gt2, fg, wg, wu, wd, ys, seq, tm):
    t, d = x1.shape
    f = wg.shape[1]
    nt = t // tm
    per_b = seq // tm
    const = lambda shape: pl.BlockSpec(shape, lambda i: (0, 0))
    return pl.pallas_call(
        _combine_kernel,
        grid=(nt,),
        in_specs=[pl.BlockSpec((tm * TOP_K,), lambda i: (i,), memory_space=pltpu.SMEM),
                  pl.BlockSpec((tm * TOP_K,), lambda i: (jnp.minimum(i + 1, nt - 1),), memory_space=pltpu.SMEM),
                  pl.BlockSpec((tm, TOP_K), lambda i: (i, 0)),
                  pl.BlockSpec((tm, d), lambda i: (i, 0)),
                  pl.BlockSpec((tm, d), lambda i: (i, 0)),
                  pl.BlockSpec((1, 1, d), lambda i: (i // per_b, 0, 0)),
                  const((1, d)), const((d, f)), const((d, f)), const((f, d)),
                  pl.BlockSpec(memory_space=pl.ANY)],
        out_specs=pl.BlockSpec((tm, d), lambda i: (i, 0)),
        out_shape=jax.ShapeDtypeStruct((t, d), F32),
        scratch_shapes=[pltpu.VMEM((TOP_K, tm * ROW_SUBLANES, 128), I32),
                        pltpu.VMEM((TOP_K, tm * ROW_SUBLANES, 128), I32), pltpu.VMEM((tm, d), F32),
                        pltpu.SemaphoreType.DMA((2,))],
        compiler_params=_params(("arbitrary",), 56),
    )(dest, dest, w_tok, x1, h2, gt2, fg, wg, wu, wd, ys)


def _block_tables(pends, row_ends, n_blocks):
    bm = EXPERT_ROWS
    n_used = (pends[-1] // bm).astype(I32)
    j = jnp.arange(n_blocks, dtype=I32)
    jc = jnp.minimum(j, n_used - 1)
    block_e = jnp.sum((pends[None, :] <= (jc * bm)[:, None]).astype(I32), axis=1)
    rows = jnp.where(j < n_used, jnp.clip(row_ends[block_e] - j * bm, 0, bm), 0).astype(I32)
    prev_e = jnp.concatenate([jnp.full((1,), -1, I32), block_e[:-1]])
    first = ((block_e != prev_e) & (j < n_used)).astype(I32)
    slot = (jnp.cumsum(first) - 1) % WEIGHT_SLOTS
    big = jnp.int32(n_blocks)
    later = lax.cummin(jnp.where(first == 1, j, big)[::-1])[::-1]
    nxt = jnp.concatenate([later[1:], big[None]])
    hop = lambda at: jnp.where(at < big, nxt[jnp.minimum(at, n_blocks - 1)], big)
    expert_at = lambda at: jnp.where(at < big, block_e[jnp.minimum(at, n_blocks - 1)], -1).astype(I32)
    ahead = j
    warm = []
    for _ in range(WEIGHT_SLOTS - 1):
        warm.append(expert_at(ahead[0]))
        ahead = hop(ahead)
    return block_e, rows, n_used[None], first, slot.astype(I32), expert_at(ahead), jnp.stack(warm)


def kernel(x, c, w_ada, b_ada, norm1_g, w_in, conv_w, conv_b, lru_wa, lru_ba, lru_wx, lru_bx,
           lru_lambda, hgrn_lb, hgrn_norm_g, w_out, norm2_g, w_router, router_bias,
           w_gate, w_up, w_down, ws_gate, ws_up, ws_down, final_g):
    batch, seq, d = x.shape
    t = batch * seq
    depth = w_ada.shape[0]
    d_lru = conv_w.shape[2]
    d_hgrn = d - d_lru
    heads = d_hgrn // HEAD_DIM
    ne = w_router.shape[2]
    assert d_lru % LRU_BLOCK == 0 and d_hgrn % HEAD_DIM == 0 and seq % 512 == 0 and ne % (8 * N_GROUPS) == 0
    assert depth == 1, "the final norm is fused into the combine kernel"
    assert d == 2 * 128 * ROW_SUBLANES, "a packed row must be exactly one (ROW_SUBLANES, 128) tile"
    l = 0

    lower_bounds = jnp.cumsum(jax.nn.softmax(hgrn_lb.astype(F32), axis=0), axis=0)
    c_pad = jnp.pad(c, ((0, -batch % 8), (0, 0)))
    mod = _adaln_mod(c_pad, w_ada[l], b_ada[l][None, :])[:batch]
    sh1, sc1, gt1, sh2, sc2, gt2 = [m[:, None, :] for m in jnp.split(mod, 6, axis=-1)]

    proj = _norm_matmul(x, norm1_g[l][None, :], sc1, sh1, w_in[l].astype(BF16), min(1024, seq), 2048, BF16)
    y_lru = _rg_lru(proj, conv_w[l], conv_b[l][None, :], lru_wa[l].astype(BF16), lru_ba[l][None, :],
                    lru_wx[l].astype(BF16), lru_bx[l][None, :], lru_lambda[l][None, :], batch, seq, 256)
    y_hgrn = _hgrn2(proj, lower_bounds[l][None, :], hgrn_norm_g[l][None, :], batch, seq, heads,
                    2 * d_lru // HEAD_DIM, min(seq, 2048), 4)
    wo = w_out[l].astype(BF16)
    x1, h2, h2p = _out_proj(y_lru, y_hgrn, wo[:d_lru], wo[d_lru:], x, gt1, norm2_g[l][None, :], sc2, sh2, 512)

    idx, w_sel, rank, counts = _route(h2, w_router[l].T.astype(BF16), router_bias[l][:, None], 512)
    counts = counts[:, 0]
    bm = EXPERT_ROWS
    pcounts = (counts + bm - 1) // bm * bm
    pends = jnp.cumsum(pcounts)
    pstarts = pends - pcounts
    n_rows = -(-(t * TOP_K + ne * (bm - 1)) // bm) * bm
    dest = _slots(pstarts, idx, rank, min(2048, t)).T.reshape(-1)
    xs = _dispatch(counts, pstarts, pcounts, dest, h2p, n_rows, 256)
    ys = _experts(*_block_tables(pends, pstarts + counts, n_rows // bm), xs, w_gate[l], w_up[l], w_down[l])
    out = _combine(dest, w_sel.T, x1, h2, gt2, final_g[None, :], ws_gate[l].astype(BF16),
                   ws_up[l].astype(BF16), ws_down[l].astype(BF16), ys, seq, 256)
    return out.reshape(batch, seq, d)
```

```python
import functools

import jax
import jax.numpy as jnp
from jax import lax
from jax.experimental import pallas as pl
from jax.experimental.pallas import tpu as pltpu

F32 = jnp.float32
BF16 = jnp.bfloat16
I32 = jnp.int32

RMS_EPS = 1e-6
LRU_C = 8.0
LRU_BLOCK = 128
CONV_WIDTH = 4
HEAD_DIM = 128
CHUNK = 64
SUB = 16
N_GROUPS = 8
TOPK_GROUPS = 4
TOP_K = 8
ROUTED_SCALE = 2.5
EXPERT_ROWS = 256
ROW_SUBLANES = 8
WEIGHT_SLOTS = 3
MIB = 1024 * 1024


def _params(sem, vmem_mib):
    return pltpu.CompilerParams(dimension_semantics=sem, vmem_limit_bytes=vmem_mib * MIB)


def _sigmoid(x):
    return 0.5 * jnp.tanh(0.5 * x) + 0.5


def _dot(a, b):
    return jnp.dot(a, b, preferred_element_type=F32)


def _dot_nt(a, b):
    return lax.dot_general(a, b, (((1,), (1,)), ((), ())), preferred_element_type=F32)


def _dot_tn(a, b):
    return lax.dot_general(a, b, (((0,), (0,)), ((), ())), preferred_element_type=F32)


def _mod_kernel(c_ref, w_ref, b_ref, o_ref):
    c = c_ref[...]
    cond = c * _sigmoid(c)
    o_ref[...] = _dot(cond.astype(BF16), w_ref[...].astype(BF16)) + b_ref[...]


def _adaln_mod(c_pad, w_ada, b_ada):
    rows, d = c_pad.shape
    n = w_ada.shape[1]
    bn = 1024
    return pl.pallas_call(
        _mod_kernel,
        grid=(n // bn,),
        in_specs=[pl.BlockSpec((rows, d), lambda j: (0, 0)),
                  pl.BlockSpec((d, bn), lambda j: (0, j)),
                  pl.BlockSpec((1, bn), lambda j: (0, j))],
        out_specs=pl.BlockSpec((rows, bn), lambda j: (0, j)),
        out_shape=jax.ShapeDtypeStruct((rows, n), F32),
        compiler_params=_params(("arbitrary",), 40),
    )(c_pad, w_ada, b_ada)


def _norm_mm_kernel(x_ref, g_ref, sc_ref, sh_ref, w_ref, o_ref):
    x = x_ref[0]
    ms = jnp.mean(x * x, axis=-1, keepdims=True)
    y = x * lax.rsqrt(ms + RMS_EPS) * g_ref[...]
    h = (y * (1.0 + sc_ref[0]) + sh_ref[0]).astype(BF16)
    o_ref[...] = _dot(h, w_ref[...]).astype(o_ref.dtype)


def _norm_matmul(x, g, sc, sh, w, bm, bn, out_dtype):
    b, l, k = x.shape
    n = w.shape[1]
    nt = l // bm
    bvec = lambda: pl.BlockSpec((1, 1, k), lambda j, i, t: (i, 0, 0))
    return pl.pallas_call(
        _norm_mm_kernel,
        grid=(n // bn, b, nt),
        in_specs=[pl.BlockSpec((1, bm, k), lambda j, i, t: (i, t, 0)),
                  pl.BlockSpec((1, k), lambda j, i, t: (0, 0)),
                  bvec(), bvec(),
                  pl.BlockSpec((k, bn), lambda j, i, t: (0, j))],
        out_specs=pl.BlockSpec((bm, bn), lambda j, i, t: (i * nt + t, j)),
        out_shape=jax.ShapeDtypeStruct((b * l, n), out_dtype),
        compiler_params=_params(("arbitrary", "arbitrary", "arbitrary"), 56),
    )(x, g, sc, sh, w)


def _shift_rows(u, prev, k):
    if k == 0:
        return u
    tt, c = u.shape
    rot = pltpu.roll(u.reshape(tt // 8, 8, c), k, 1)
    before = jnp.concatenate([pltpu.roll(prev, k, 0)[None], rot[:-1]], axis=0)
    r = lax.broadcasted_iota(I32, rot.shape, 1)
    return jnp.where(r < k, before, rot).reshape(tt, c)


def _lru_kernel(u_ref, z_ref, cw_ref, cb_ref, wa_ref, ba_ref, wx_ref, bx_ref, lam_ref,
                o_ref, prev_ref, h_ref):
    j = pl.program_id(1)
    tt, c = u_ref.shape

    @pl.when(j == 0)
    def _():
        prev_ref[...] = jnp.zeros_like(prev_ref)
        h_ref[...] = jnp.zeros_like(h_ref)

    u = u_ref[...].astype(F32)
    prev = prev_ref[...]
    xc = jnp.broadcast_to(cb_ref[...], (tt, c))
    for w in range(CONV_WIDTH):
        xc = xc + cw_ref[w:w + 1, :] * _shift_rows(u, prev, CONV_WIDTH - 1 - w)
    prev_ref[...] = u[tt - 8:, :]

    xcb = xc.astype(BF16)
    ra, rx = [], []
    for blk in range(c // LRU_BLOCK):
        xs = xcb[:, blk * LRU_BLOCK:(blk + 1) * LRU_BLOCK]
        ra.append(_dot(xs, wa_ref[blk]))
        rx.append(_dot(xs, wx_ref[blk]))
    ig = _sigmoid(jnp.concatenate(rx, axis=1) + bx_ref[...])

    nl = -lam_ref[...]
    softplus = jnp.maximum(nl, 0.0) + jnp.log1p(jnp.exp(-jnp.abs(nl)))
    half_rate = (-0.5 * LRU_C) * softplus
    log_a = half_rate * jnp.tanh(0.5 * (jnp.concatenate(ra, axis=1) + ba_ref[...])) + half_rate
    a = jnp.exp(log_a)
    mult = jnp.sqrt(1.0 - a * a)
    row = lax.broadcasted_iota(I32, (tt, c), 0)
    mult = jnp.where((row == 0) & (j == 0), 1.0, mult)
    bv = mult * ig * xc

    a3 = a.reshape(tt // 8, 8, c)
    b3 = bv.reshape(tt // 8, 8, c)
    in_group = lax.broadcasted_iota(I32, a3.shape, 1)
    s = 1
    while s < 8:
        keep = in_group >= s
        b3 = jnp.where(keep, a3 * pltpu.roll(b3, s, 1) + b3, b3)
        a3 = jnp.where(keep, a3 * pltpu.roll(a3, s, 1), a3)
        s *= 2
    carry = h_ref[...]
    groups = []
    for gi in range(tt // 8):
        hg = b3[gi] + a3[gi] * carry
        carry = hg[7:8, :]
        groups.append(hg)
    h = jnp.concatenate(groups, axis=0)
    h_ref[...] = carry

    z = z_ref[...].astype(F32)
    gelu = (0.5 * z) * (1.0 + jnp.tanh(z * (0.7978845608028654 + 0.035677408136300125 * (z * z))))
    o_ref[...] = (h * gelu).astype(o_ref.dtype)


def _rg_lru(proj, conv_w, conv_b, wa, ba, wx, bx, lam, batch, seq, tt):
    t = proj.shape[0]
    c = conv_w.shape[1]
    nt = seq // tt
    vec = lambda: pl.BlockSpec((1, c), lambda i, j: (0, 0))
    mat = lambda: pl.BlockSpec(wa.shape, lambda i, j: (0, 0, 0))
    return pl.pallas_call(
        _lru_kernel,
        grid=(batch, nt),
        in_specs=[pl.BlockSpec((tt, c), lambda i, j: (i * nt + j, 0)),
                  pl.BlockSpec((tt, c), lambda i, j: (i * nt + j, 1)),
                  pl.BlockSpec((CONV_WIDTH, c), lambda i, j: (0, 0)),
                  vec(), mat(), vec(), mat(), vec(), vec()],
        out_specs=pl.BlockSpec((tt, c), lambda i, j: (i * nt + j, 0)),
        out_shape=jax.ShapeDtypeStruct((t, c), BF16),
        scratch_shapes=[pltpu.VMEM((8, c), F32), pltpu.VMEM((1, c), F32)],
        compiler_params=_params(("arbitrary", "arbitrary"), 48),
    )(proj, proj, conv_w, conv_b, wa, ba, wx, bx, lam)


def _hgrn_lanes(wid):
    return [slice(h * HEAD_DIM, (h + 1) * HEAD_DIM) for h in range(wid // HEAD_DIM)]


def _per_head(wid, fn):
    return jnp.concatenate([fn(h, ln) for h, ln in enumerate(_hgrn_lanes(wid))], axis=1)


def _row_on_sublanes(slab_ref, r):
    return jnp.concatenate([slab_ref[h, pl.ds(r, 8, stride=0), :] for h in range(slab_ref.shape[0])], axis=1)


_N_SUB = CHUNK // SUB
_STACK_ROWS = 8 * 16 + 8 * 8
_KT_OFFSET = [SUB * i * (i - 1) // 2 for i in range(_N_SUB + 1)]


def _hgrn_front_shapes(wid):
    return [pltpu.VMEM((CHUNK, wid), BF16),
            pltpu.VMEM((CHUNK, wid), BF16),
            pltpu.VMEM((CHUNK, wid), BF16),
            pltpu.VMEM((CHUNK, wid), BF16),
            pltpu.VMEM((1, wid), F32),
            pltpu.VMEM((CHUNK, wid), F32),
            pltpu.VMEM((wid // HEAD_DIM, CHUNK, HEAD_DIM), F32),
            pltpu.VMEM((wid // HEAD_DIM, CHUNK, HEAD_DIM), F32),
            pltpu.VMEM((_KT_OFFSET[_N_SUB], wid), BF16),
            pltpu.VMEM((_N_SUB * _STACK_ROWS, wid), BF16)]


def _hgrn_front(qi, fi, v, gi, lb, out):
    qe_ref, qt_ref, vb_ref, kd_ref, decay_ref, gate_ref, v_ref, c_ref, kt_ref, stack_ref = out
    n, wid = qi.shape
    q = qi * _sigmoid(qi)
    f = lb + (1.0 - lb) * _sigmoid(fi)
    row = lax.broadcasted_iota(I32, (n, wid), 0)
    b = jnp.log2(f)
    s = 1
    while s < n:
        b = b + jnp.where(row >= s, pltpu.roll(b, s, 0), 0.0)
        s *= 2
    c = b - jnp.log2(1.0 - f)
    qe_ref[...] = (q * jnp.exp2(b)).astype(BF16)
    refb = jnp.concatenate(
        [jnp.broadcast_to(b[i * SUB:i * SUB + 1, :], (SUB, wid)) for i in range(_N_SUB)], axis=0)
    qt_ref[...] = (q * jnp.exp2(b - refb)).astype(BF16)
    vb_ref[...] = v.astype(BF16)
    for h, ln in enumerate(_hgrn_lanes(wid)):
        v_ref[h] = v[:, ln]
        c_ref[h] = c[:, ln]
    gate_ref[...] = gi * _sigmoid(gi)
    b_last = b[n - 1:n, :]
    kd_ref[...] = jnp.exp2(b_last - c).astype(BF16)
    decay_ref[...] = jnp.exp2(b_last)
    row8 = lax.broadcasted_iota(I32, (8, wid), 0)
    for i in range(_N_SUB):
        lo = i * SUB
        if i > 0:
            kt_ref[_KT_OFFSET[i]:_KT_OFFSET[i + 1], :] = jnp.exp2(b[lo:lo + 1, :] - c[:lo, :]).astype(BF16)
        q0, q1 = q[lo:lo + 8, :], q[lo + 8:lo + SUB, :]
        b0, b1 = b[lo:lo + 8, :], b[lo + 8:lo + SUB, :]
        base = i * _STACK_ROWS
        pending = None
        for s_ in range(SUB):
            cs = _row_on_sublanes(c_ref, lo + s_)
            p1 = q1 * jnp.exp2(b1 - cs)
            if s_ < 8:
                pair = [jnp.where(row8 >= s_, q0 * jnp.exp2(b0 - cs), 0.0), p1]
                at = base + 16 * s_
            elif pending is None:
                pending = jnp.where(row8 >= s_ - 8, p1, 0.0)
                continue
            else:
                pair = [pending, jnp.where(row8 >= s_ - 8, p1, 0.0)]
                pending = None
                at = base + 64 + 8 * (s_ - 1)
            stack_ref[at:at + 16, :] = jnp.concatenate(pair, axis=0).astype(BF16)


def _hgrn_products(front, states, ones_b):
    qe_ref, qt_ref, _, _, _, _, _, _, kt_ref, stack_ref = front
    wid = qe_ref.shape[1]
    lanes = _hgrn_lanes(wid)
    o_state = _per_head(wid, lambda h, ln: _dot_nt(qe_ref[:, ln], states[h].astype(BF16)))
    scores = [[_dot_nt(qt_ref[i * SUB:(i + 1) * SUB, ln], kt_ref[_KT_OFFSET[i]:_KT_OFFSET[i + 1], ln]).astype(BF16)
               for ln in lanes] for i in range(1, _N_SUB)]
    reds = [_per_head(wid, lambda h, ln: _dot(stack_ref[i * _STACK_ROWS:(i + 1) * _STACK_ROWS, ln], ones_b))
            for i in range(_N_SUB)]
    return o_state, scores, reds


def _hgrn_finish(front, products, states, ng):
    _, _, vb_ref, kd_ref, decay_ref, gate_ref, v_ref, _, _, _ = front
    o_state, scores, reds = products
    n, wid = vb_ref.shape
    outs = []
    for i in range(_N_SUB):
        lo = i * SUB
        acc = o_state[lo:lo + SUB, :]
        if i > 0:
            sc = scores[i - 1]
            acc = acc + _per_head(wid, lambda h, ln: _dot(sc[h], vb_ref[:lo, ln]))
        red = reds[i]
        acc0 = jnp.zeros((8, wid), F32)
        acc1 = jnp.zeros((8, wid), F32)
        for s_ in range(SUB):
            vs = _row_on_sublanes(v_ref, lo + s_)
            if s_ < 8:
                acc0 = acc0 + red[16 * s_:16 * s_ + 8, :] * vs
                acc1 = acc1 + red[16 * s_ + 8:16 * s_ + 16, :] * vs
            else:
                acc1 = acc1 + red[64 + 8 * s_:72 + 8 * s_, :] * vs
        outs.append(acc + jnp.concatenate([acc0, acc1], axis=0))
    o = jnp.concatenate(outs, axis=0)
    decay = decay_ref[...]
    new_states = [states[h] * decay[:, ln] + _dot_tn(vb_ref[:, ln], kd_ref[:, ln])
                  for h, ln in enumerate(_hgrn_lanes(wid))]
    oo = o * o
    scale = _per_head(wid, lambda h, ln: jnp.broadcast_to(
        lax.rsqrt(jnp.mean(oo[:, ln], axis=-1, keepdims=True) + RMS_EPS), (n, HEAD_DIM)) * ng)
    return o * scale * gate_ref[...], new_states


def _hgrn_kernel(q_ref, f_ref, v_ref, g_ref, lb_ref, ng_ref, o_ref, st_ref, *front_refs):
    @pl.when(pl.program_id(2) == 0)
    def _():
        st_ref[...] = jnp.zeros_like(st_ref)

    lb = lb_ref[...]
    ng = ng_ref[...]
    heads = q_ref.shape[1] // HEAD_DIM
    n_chunks = q_ref.shape[0] // CHUNK
    ones_b = jnp.ones((HEAD_DIM, HEAD_DIM), BF16)
    half = len(front_refs) // 2
    fronts = (front_refs[:half], front_refs[half:])

    def chunk_rows(ci):
        return pl.ds(pl.multiple_of(ci * CHUNK, CHUNK), CHUNK)

    def front(ci, out):
        rows = chunk_rows(ci)
        _hgrn_front(q_ref[rows, :].astype(F32), f_ref[rows, :].astype(F32),
                    v_ref[rows, :].astype(F32), g_ref[rows, :].astype(F32), lb, out)

    def back(ci, cur, nxt):
        states = [st_ref[h] for h in range(heads)]
        products = _hgrn_products(cur, states, ones_b)
        front(jnp.minimum(ci + 1, n_chunks - 1), nxt)
        y, new_states = _hgrn_finish(cur, products, states, ng)
        for h in range(heads):
            st_ref[h] = new_states[h]
        o_ref[chunk_rows(ci), :] = y.astype(o_ref.dtype)

    def body(pair, carry):
        back(2 * pair, fronts[0], fronts[1])
        back(2 * pair + 1, fronts[1], fronts[0])
        return carry

    front(0, fronts[0])
    lax.fori_loop(0, n_chunks // 2, body, 0)


def _hgrn2(proj, lb, norm_g, batch, seq, heads, col0, tt, hp):
    t = proj.shape[0]
    nt = seq // tt
    wid = hp * HEAD_DIM
    assert heads % hp == 0 and col0 % hp == 0
    col = lambda off: pl.BlockSpec((tt, wid), lambda i, h, j: (i * nt + j, (col0 + off * heads) // hp + h))
    return pl.pallas_call(
        _hgrn_kernel,
        grid=(batch, heads // hp, nt),
        in_specs=[col(0), col(1), col(2), col(3),
                  pl.BlockSpec((1, wid), lambda i, h, j: (0, h)),
                  pl.BlockSpec((1, HEAD_DIM), lambda i, h, j: (0, 0))],
        out_specs=pl.BlockSpec((tt, wid), lambda i, h, j: (i * nt + j, h)),
        out_shape=jax.ShapeDtypeStruct((t, heads * HEAD_DIM), BF16),
        scratch_shapes=[pltpu.VMEM((hp, HEAD_DIM, HEAD_DIM), F32)] + 2 * _hgrn_front_shapes(wid),
        compiler_params=_params(("arbitrary", "arbitrary", "arbitrary"), 48),
    )(proj, proj, proj, proj, lb, norm_g)


def _pack_halves(y):
    n = y.shape[1] // 2
    lo = lax.bitcast_convert_type(y[:, :n].astype(BF16).astype(F32), I32)
    hi = lax.bitcast_convert_type(y[:, n:].astype(BF16).astype(F32), I32)
    return (hi & -65536) | lax.shift_right_logical(lo, 16)


def _unpack_halves(p):
    lo = lax.bitcast_convert_type(lax.shift_left(p, 16), F32)
    hi = lax.bitcast_convert_type(p & -65536, F32)
    return jnp.concatenate([lo, hi], axis=1)


def _store_row_tiles(ref, packed):
    m = packed.shape[0]
    for s in range(ROW_SUBLANES):
        ref[pl.ds(s, m, stride=ROW_SUBLANES), :] = packed[:, s * 128:(s + 1) * 128]


def _load_row_tiles(ref, m):
    return jnp.concatenate([ref[pl.ds(s, m, stride=ROW_SUBLANES), :] for s in range(ROW_SUBLANES)], axis=1)


def _out_kernel(yl_ref, yh_ref, wl_ref, wh_ref, x_ref, gt_ref, g2_ref, sc_ref, sh_ref, x1_ref, h2_ref, h2p_ref):
    mix = _dot(yl_ref[...], wl_ref[...]) + _dot(yh_ref[...], wh_ref[...])
    x1 = x_ref[0] + gt_ref[0] * mix
    x1_ref[...] = x1
    ms = jnp.mean(x1 * x1, axis=-1, keepdims=True)
    y = x1 * lax.rsqrt(ms + RMS_EPS) * g2_ref[...]
    h2 = y * (1.0 + sc_ref[0]) + sh_ref[0]
    h2_ref[...] = h2.astype(BF16)
    _store_row_tiles(h2p_ref, _pack_halves(h2))


def _out_proj(yl, yh, wl, wh, x, gt1, g2, sc2, sh2, bm):
    b, l, d = x.shape
    t = b * l
    nt = l // bm
    cl, ch = yl.shape[1], yh.shape[1]
    bvec = lambda: pl.BlockSpec((1, 1, d), lambda i, j: (i, 0, 0))
    return pl.pallas_call(
        _out_kernel,
        grid=(b, nt),
        in_specs=[pl.BlockSpec((bm, cl), lambda i, j: (i * nt + j, 0)),
                  pl.BlockSpec((bm, ch), lambda i, j: (i * nt + j, 0)),
                  pl.BlockSpec((cl, d), lambda i, j: (0, 0)),
                  pl.BlockSpec((ch, d), lambda i, j: (0, 0)),
                  pl.BlockSpec((1, bm, d), lambda i, j: (i, j, 0)),
                  bvec(),
                  pl.BlockSpec((1, d), lambda i, j: (0, 0)),
                  bvec(), bvec()],
        out_specs=[pl.BlockSpec((bm, d), lambda i, j: (i * nt + j, 0)),
                   pl.BlockSpec((bm, d), lambda i, j: (i * nt + j, 0)),
                   pl.BlockSpec((bm * ROW_SUBLANES, 128), lambda i, j: (i * nt + j, 0))],
        out_shape=[jax.ShapeDtypeStruct((t, d), F32), jax.ShapeDtypeStruct((t, d), BF16),
                   jax.ShapeDtypeStruct((t * ROW_SUBLANES, 128), I32)],
        compiler_params=_params(("arbitrary", "arbitrary"), 56),
    )(yl, yh, wl, wh, x, gt1, g2, sc2, sh2)


def _route_kernel(h_ref, wr_ref, bias_ref, idx_ref, w_ref, rank_ref, cnt_ref, carry_ref):
    step = pl.program_id(0)
    tm = h_ref.shape[0]
    ne = wr_ref.shape[0]
    gsz = ne // N_GROUPS
    neg = -jnp.inf

    @pl.when(step == 0)
    def _():
        carry_ref[...] = jnp.zeros_like(carry_ref)

    logits = _dot_nt(wr_ref[...], h_ref[...])
    scores = _sigmoid(logits)
    sel = scores + bias_ref[...]

    sel3 = sel.reshape(N_GROUPS, gsz, tm)
    pos3 = lax.broadcasted_iota(I32, (N_GROUPS, gsz, tm), 1)
    m1 = jnp.max(sel3, axis=1, keepdims=True)
    i1 = jnp.min(jnp.where(sel3 == m1, pos3, gsz), axis=1, keepdims=True)
    m2 = jnp.max(jnp.where(pos3 == i1, neg, sel3), axis=1, keepdims=True)
    gs = (m1 + m2).reshape(N_GROUPS, tm)

    gidx = lax.broadcasted_iota(I32, (N_GROUPS, tm), 0)
    beaten = jnp.zeros((N_GROUPS, tm), I32)
    for gp in range(N_GROUPS):
        other = gs[gp:gp + 1, :]
        beats = (other > gs) | ((other == gs) & (gp < gidx))
        beaten = beaten + beats.astype(I32)
    gkeep = (beaten < TOPK_GROUPS).reshape(N_GROUPS, 1, tm)
    cur = jnp.where(gkeep, sel3, neg).reshape(ne, tm)

    eidx = lax.broadcasted_iota(I32, (ne, tm), 0)
    picked = jnp.zeros((ne, tm), jnp.bool_)
    idx_rows, w_rows = [], []
    for _ in range(TOP_K):
        m = jnp.max(cur, axis=0, keepdims=True)
        ik = jnp.min(jnp.where(cur == m, eidx, ne), axis=0, keepdims=True)
        hit = eidx == ik
        w_rows.append(jnp.sum(jnp.where(hit, scores, 0.0), axis=0, keepdims=True))
        idx_rows.append(ik)
        cur = jnp.where(hit, neg, cur)
        picked = picked | hit
    w = jnp.concatenate(w_rows, axis=0)
    idx_ref[...] = jnp.concatenate(idx_rows, axis=0)
    w_ref[...] = w / jnp.sum(w, axis=0, keepdims=True) * ROUTED_SCALE

    pf = picked.astype(F32)
    ta = lax.broadcasted_iota(I32, (tm, tm), 0)
    tb = lax.broadcasted_iota(I32, (tm, tm), 1)
    before = (ta < tb).astype(BF16)
    cnt = _dot(pf.astype(BF16), before) + carry_ref[...]
    rank_rows = [jnp.sum(jnp.where(eidx == ik, cnt, 0.0), axis=0, keepdims=True) for ik in idx_rows]
    rank_ref[...] = jnp.concatenate(rank_rows, axis=0).astype(I32)
    total = carry_ref[...] + jnp.sum(pf, axis=1, keepdims=True)
    carry_ref[...] = total
    cnt_ref[...] = jnp.broadcast_to(total, cnt_ref.shape).astype(I32)


def _route(h2, wr_t, bias_col, tm):
    t, d = h2.shape
    ne = wr_t.shape[0]
    row = lambda: pl.BlockSpec((TOP_K, tm), lambda i: (0, i))
    return pl.pallas_call(
        _route_kernel,
        grid=(t // tm,),
        in_specs=[pl.BlockSpec((tm, d), lambda i: (i, 0)),
                  pl.BlockSpec((ne, d), lambda i: (0, 0)),
                  pl.BlockSpec((ne, 1), lambda i: (0, 0))],
        out_specs=[row(), row(), row(), pl.BlockSpec((ne, 128), lambda i: (0, 0))],
        out_shape=[jax.ShapeDtypeStruct((TOP_K, t), I32), jax.ShapeDtypeStruct((TOP_K, t), F32),
                   jax.ShapeDtypeStruct((TOP_K, t), I32), jax.ShapeDtypeStruct((ne, 128), I32)],
        scratch_shapes=[pltpu.VMEM((ne, 1), F32)],
        compiler_params=_params(("arbitrary",), 40),
    )(h2, wr_t, bias_col)


def _slots_kernel(ps_ref, idx_ref, rank_ref, o_ref):
    idx = idx_ref[...]

    def body(e, acc):
        return jnp.where(idx == e, ps_ref[e], acc)

    row = lax.fori_loop(0, ps_ref.shape[0], body, jnp.zeros(idx.shape, I32)) + rank_ref[...]
    o_ref[...] = row * ROW_SUBLANES


def _slots(pstarts, idx, rank, tm):
    k, t = idx.shape
    blk = lambda: pl.BlockSpec((k, tm), lambda i, ps: (0, i))
    return pl.pallas_call(
        _slots_kernel,
        grid_spec=pltpu.PrefetchScalarGridSpec(
            num_scalar_prefetch=1, grid=(t // tm,), in_specs=[blk(), blk()], out_specs=blk()),
        out_shape=jax.ShapeDtypeStruct((k, t), I32),
        compiler_params=_params(("arbitrary",), 32),
    )(pstarts, idx, rank)


def _pad_bits():
    bit = EXPERT_ROWS // 2
    while bit >= 1:
        yield bit
        bit //= 2


def _row_tile(ref, first_sublane_row, rows=1):
    return ref.at[pl.ds(pl.multiple_of(first_sublane_row, ROW_SUBLANES), rows * ROW_SUBLANES), :]


def _dispatch_kernel(cnt_ref, ps_ref, pc_ref, dest_ref, h_ref, xs_ref, zero_ref, sem, pad_sem):
    tm = h_ref.shape[0] // ROW_SUBLANES

    def copy(t, k):
        return pltpu.make_async_copy(_row_tile(h_ref, t * ROW_SUBLANES),
                                     _row_tile(xs_ref, dest_ref[t * TOP_K + k]), sem)

    def start(t, carry):
        for k in range(TOP_K):
            copy(t, k).start(priority=k % 2)
        return carry

    def wait_all():
        for _ in range(TOP_K):
            pltpu.make_async_copy(h_ref, xs_ref.at[pl.ds(0, tm * ROW_SUBLANES), :], sem).wait()

    lax.fori_loop(0, tm, start, 0)

    @pl.when(pl.program_id(0) == 0)
    def _():
        zero_ref[...] = jnp.zeros_like(zero_ref)

        def pad_copy(first, rows):
            return pltpu.make_async_copy(_row_tile(zero_ref, 0, rows),
                                         _row_tile(xs_ref, first * ROW_SUBLANES, rows), pad_sem)

        def pads(e, fn):
            pad = pc_ref[e] - cnt_ref[e]
            lo = ps_ref[e] + cnt_ref[e]
            for bit in _pad_bits():
                @pl.when((pad & bit) != 0)
                def _():
                    fn(pad_copy(lo + (pad // (2 * bit)) * (2 * bit), bit))

        def start_pads(e, carry):
            pads(e, lambda cp: cp.start())
            return carry

        def wait_pads(e, carry):
            pads(e, lambda cp: cp.wait())
            return carry

        lax.fori_loop(0, cnt_ref.shape[0], start_pads, 0)
        lax.fori_loop(0, cnt_ref.shape[0], wait_pads, 0)

    wait_all()


def _dispatch(counts, pstarts, pcounts, dest, h2p, n_rows, tm):
    t = h2p.shape[0] // ROW_SUBLANES
    return pl.pallas_call(
        _dispatch_kernel,
        grid_spec=pltpu.PrefetchScalarGridSpec(
            num_scalar_prefetch=3,
            grid=(t // tm,),
            in_specs=[pl.BlockSpec((tm * TOP_K,), lambda i, *_: (i,), memory_space=pltpu.SMEM),
                      pl.BlockSpec((tm * ROW_SUBLANES, 128), lambda i, *_: (i, 0))],
            out_specs=pl.BlockSpec(memory_space=pl.ANY),
            scratch_shapes=[pltpu.VMEM((EXPERT_ROWS // 2 * ROW_SUBLANES, 128), I32),
                            pltpu.SemaphoreType.DMA, pltpu.SemaphoreType.DMA],
        ),
        out_shape=jax.ShapeDtypeStruct((n_rows * ROW_SUBLANES, 128), I32),
        compiler_params=_params(("arbitrary",), 32),
    )(counts, pstarts, pcounts, dest, h2p)


def _expert_kernel(be_ref, rows_ref, nu_ref, first_ref, slot_ref, ahead_ref, warm_ref, x_ref, wg_hbm, wu_hbm, wd_hbm,
                   o_ref, wgf, wuf, wdf, wgb, wub, wdb, sem):
    nb = pl.program_id(0)
    half_f = wd_hbm.shape[1] // 2

    def fetch(e, s):
        lo, hi = pl.ds(0, half_f), pl.ds(half_f, half_f)
        return (pltpu.make_async_copy(wg_hbm.at[e], wgf.at[s], sem.at[s, 0]),
                pltpu.make_async_copy(wu_hbm.at[e], wuf.at[s], sem.at[s, 1]),
                pltpu.make_async_copy(wd_hbm.at[e, lo], wdf.at[s, lo], sem.at[s, 2]),
                pltpu.make_async_copy(wd_hbm.at[e, hi], wdf.at[s, hi], sem.at[s, 3]))

    def start(e, s):
        for i, cp in enumerate(fetch(e, s)):
            cp.start(priority=i % 2)

    @pl.when(nb == 0)
    def _():
        for i in range(WEIGHT_SLOTS - 1):
            @pl.when(warm_ref[i] >= 0)
            def _():
                start(warm_ref[i], i)

    @pl.when(first_ref[nb] == 1)
    def _():
        s = slot_ref[nb]
        for cp in fetch(be_ref[nb], s):
            cp.wait()

        @pl.when(ahead_ref[nb] >= 0)
        def _():
            start(ahead_ref[nb], (s + WEIGHT_SLOTS - 1) % WEIGHT_SLOTS)

    def mlp(rows, opening):
        if opening:
            s = slot_ref[nb]
            wg, wu, wd = wgf[s].astype(BF16), wuf[s].astype(BF16), wdf[s].astype(BF16)
            wgb[...], wub[...], wdb[...] = wg, wu, wd
        else:
            wg, wu, wd = wgb[...], wub[...], wdb[...]
        x = _unpack_halves(_load_row_tiles(x_ref, rows)).astype(BF16)
        g = _dot(x, wg)
        u = _dot(x, wu)
        a = (g * _sigmoid(g)) * u
        _store_row_tiles(o_ref, _pack_halves(_dot(a.astype(BF16), wd)))
        if rows < EXPERT_ROWS:
            o_ref[rows * ROW_SUBLANES:, :] = jnp.zeros(((EXPERT_ROWS - rows) * ROW_SUBLANES, 128), I32)

    half = EXPERT_ROWS // 2
    n_real = rows_ref[nb]
    for opening in (True, False):
        is_kind = (first_ref[nb] == 1) == opening
        pl.when(is_kind & (n_real > half))(functools.partial(mlp, EXPERT_ROWS, opening))
        pl.when(is_kind & (n_real > 0) & (n_real <= half))(functools.partial(mlp, half, opening))


def _experts(block_e, rows, n_used, first, slot, ahead_e, warm_e, xs, w_gate, w_up, w_down):
    r, dp = xs.shape
    ne, d, f = w_gate.shape
    bm = EXPERT_ROWS * ROW_SUBLANES
    ws = WEIGHT_SLOTS
    row_blk = lambda nb, be, rw, nu, *_: (jnp.minimum(nb, nu[0] - 1), 0)
    return pl.pallas_call(
        _expert_kernel,
        grid_spec=pltpu.PrefetchScalarGridSpec(
            num_scalar_prefetch=7,
            grid=(r // bm,),
            in_specs=[pl.BlockSpec((bm, dp), row_blk),
                      pl.BlockSpec(memory_space=pl.ANY),
                      pl.BlockSpec(memory_space=pl.ANY),
                      pl.BlockSpec(memory_space=pl.ANY)],
            out_specs=pl.BlockSpec((bm, dp), row_blk),
            scratch_shapes=[pltpu.VMEM((ws, d, f), F32), pltpu.VMEM((ws, d, f), F32), pltpu.VMEM((ws, f, d), F32),
                            pltpu.VMEM((d, f), BF16), pltpu.VMEM((d, f), BF16), pltpu.VMEM((f, d), BF16),
                            pltpu.SemaphoreType.DMA((ws, 4))],
        ),
        out_shape=jax.ShapeDtypeStruct((r, dp), I32),
        compiler_params=_params(("arbitrary",), 60),
    )(block_e, rows, n_used, first, slot, ahead_e, warm_e, xs, w_gate, w_up, w_down)


def _combine_kernel(dest_ref, dnext_ref, w_ref, x1_ref, h_ref, gt_ref, fg_ref, wg_ref, wu_ref, wd_ref, ys_ref,
                    o_ref, buf0, buf1, routed_ref, shared_ref, sem):
    buf = (buf0, buf1)
    i = pl.program_id(0)
    last = pl.num_programs(0) - 1
    tm = x1_ref.shape[0]

    def copy(idx_ref, s, t, k):
        return pltpu.make_async_copy(_row_tile(ys_ref, idx_ref[t * TOP_K + k]),
                                     _row_tile(buf[s].at[k], t * ROW_SUBLANES), sem.at[s])

    def start_token(idx_ref, s, t):
        for k in range(TOP_K):
            copy(idx_ref, s, t, k).start(priority=k % 2)

    def wait_tile(s):
        for k in range(TOP_K):
            pltpu.make_async_copy(ys_ref.at[pl.ds(0, tm * ROW_SUBLANES), :], buf[s].at[k], sem.at[s]).wait()

    @pl.when(i == 0)
    def _():
        def body(t, carry):
            start_token(dest_ref, 0, t)
            return carry
        lax.fori_loop(0, tm, body, 0)

    def gather_reduce(slot):
        wait_tile(slot)

        def group(g, carry):
            for r in range(8):
                start_token(dnext_ref, 1 - slot, g * 8 + r)
            base = pl.multiple_of(g * (8 * ROW_SUBLANES), 8 * ROW_SUBLANES)
            w8 = w_ref[pl.ds(pl.multiple_of(g * 8, 8), 8), :]
            lo = [None] * ROW_SUBLANES
            hi = [None] * ROW_SUBLANES
            for k in range(TOP_K):
                wk = w8[:, k:k + 1]
                for s in range(ROW_SUBLANES):
                    p = buf[slot][k, pl.ds(base + s, 8, stride=ROW_SUBLANES), :]
                    l = lax.bitcast_convert_type(lax.shift_left(p, 16), F32) * wk
                    h = lax.bitcast_convert_type(p & -65536, F32) * wk
                    lo[s] = l if k == 0 else lo[s] + l
                    hi[s] = h if k == 0 else hi[s] + h
            routed_ref[pl.ds(pl.multiple_of(g * 8, 8), 8), :] = jnp.concatenate(lo + hi, axis=1)
            return carry

        x = h_ref[...]
        gate = _dot(x, wg_ref[...])
        up = _dot(x, wu_ref[...])
        shared_ref[...] = _dot(((gate * _sigmoid(gate)) * up).astype(BF16), wd_ref[...])
        for g in range(tm // 8):
            group(g, 0)

    for slot in range(2):
        pl.when(i % 2 == slot)(functools.partial(gather_reduce, slot))

    y = x1_ref[...] + gt_ref[0] * (routed_ref[...] + shared_ref[...])
    ms = jnp.mean(y * y, axis=-1, keepdims=True)
    o_ref[...] = y * lax.rsqrt(ms + RMS_EPS) * fg_ref[...]

    for slot in range(2):
        pl.when((i == last) & (i % 2 == slot))(functools.partial(wait_tile, 1 - slot))


def _combine(dest, w_tok, x1, h2, gt2, fg, wg, wu, wd, ys, seq, tm):
    t, d = x1.shape
    f = wg.shape[1]
    nt = t // tm
    per_b = seq // tm
    const = lambda shape: pl.BlockSpec(shape, lambda i: (0, 0))
    return pl.pallas_call(
        _combine_kernel,
        grid=(nt,),
        in_specs=[pl.BlockSpec((tm * TOP_K,), lambda i: (i,), memory_space=pltpu.SMEM),
                  pl.BlockSpec((tm * TOP_K,), lambda i: (jnp.minimum(i + 1, nt - 1),), memory_space=pltpu.SMEM),
                  pl.BlockSpec((tm, TOP_K), lambda i: (i, 0)),
                  pl.BlockSpec((tm, d), lambda i: (i, 0)),
                  pl.BlockSpec((tm, d), lambda i: (i, 0)),
                  pl.BlockSpec((1, 1, d), lambda i: (i // per_b, 0, 0)),
                  const((1, d)), const((d, f)), const((d, f)), const((f, d)),
                  pl.BlockSpec(memory_space=pl.ANY)],
        out_specs=pl.BlockSpec((tm, d), lambda i: (i, 0)),
        out_shape=jax.ShapeDtypeStruct((t, d), F32),
        scratch_shapes=[pltpu.VMEM((TOP_K, tm * ROW_SUBLANES, 128), I32),
                        pltpu.VMEM((TOP_K, tm * ROW_SUBLANES, 128), I32), pltpu.VMEM((tm, d), F32),
                        pltpu.VMEM((tm, d), F32),
                        pltpu.SemaphoreType.DMA((2,))],
        compiler_params=_params(("arbitrary",), 56),
    )(dest, dest, w_tok, x1, h2, gt2, fg, wg, wu, wd, ys)


def _block_tables(pends, row_ends, n_blocks):
    bm = EXPERT_ROWS
    n_used = (pends[-1] // bm).astype(I32)
    j = jnp.arange(n_blocks, dtype=I32)
    jc = jnp.minimum(j, n_used - 1)
    block_e = jnp.sum((pends[None, :] <= (jc * bm)[:, None]).astype(I32), axis=1)
    rows = jnp.where(j < n_used, jnp.clip(row_ends[block_e] - j * bm, 0, bm), 0).astype(I32)
    prev_e = jnp.concatenate([jnp.full((1,), -1, I32), block_e[:-1]])
    first = ((block_e != prev_e) & (j < n_used)).astype(I32)
    slot = (jnp.cumsum(first) - 1) % WEIGHT_SLOTS
    big = jnp.int32(n_blocks)
    later = lax.cummin(jnp.where(first == 1, j, big)[::-1])[::-1]
    nxt = jnp.concatenate([later[1:], big[None]])
    hop = lambda at: jnp.where(at < big, nxt[jnp.minimum(at, n_blocks - 1)], big)
    expert_at = lambda at: jnp.where(at < big, block_e[jnp.minimum(at, n_blocks - 1)], -1).astype(I32)
    ahead = j
    warm = []
    for _ in range(WEIGHT_SLOTS - 1):
        warm.append(expert_at(ahead[0]))
        ahead = hop(ahead)
    return block_e, rows, n_used[None], first, slot.astype(I32), expert_at(ahead), jnp.stack(warm)


def kernel(x, c, w_ada, b_ada, norm1_g, w_in, conv_w, conv_b, lru_wa, lru_ba, lru_wx, lru_bx,
           lru_lambda, hgrn_lb, hgrn_norm_g, w_out, norm2_g, w_router, router_bias,
           w_gate, w_up, w_down, ws_gate, ws_up, ws_down, final_g):
    batch, seq, d = x.shape
    t = batch * seq
    depth = w_ada.shape[0]
    d_lru = conv_w.shape[2]
    d_hgrn = d - d_lru
    heads = d_hgrn // HEAD_DIM
    ne = w_router.shape[2]
    assert d_lru % LRU_BLOCK == 0 and d_hgrn % HEAD_DIM == 0 and seq % 512 == 0 and ne % (8 * N_GROUPS) == 0
    assert depth == 1, "the final norm is fused into the combine kernel"
    assert d == 2 * 128 * ROW_SUBLANES, "a packed row must be exactly one (ROW_SUBLANES, 128) tile"
    l = 0

    lower_bounds = jnp.cumsum(jax.nn.softmax(hgrn_lb.astype(F32), axis=0), axis=0)
    c_pad = jnp.pad(c, ((0, -batch % 8), (0, 0)))
    mod = _adaln_mod(c_pad, w_ada[l], b_ada[l][None, :])[:batch]
    sh1, sc1, gt1, sh2, sc2, gt2 = [m[:, None, :] for m in jnp.split(mod, 6, axis=-1)]

    proj = _norm_matmul(x, norm1_g[l][None, :], sc1, sh1, w_in[l].astype(BF16), min(1024, seq), 2048, BF16)
    y_lru = _rg_lru(proj, conv_w[l], conv_b[l][None, :], lru_wa[l].astype(BF16), lru_ba[l][None, :],
                    lru_wx[l].astype(BF16), lru_bx[l][None, :], lru_lambda[l][None, :], batch, seq, 256)
    y_hgrn = _hgrn2(proj, lower_bounds[l][None, :], hgrn_norm_g[l][None, :], batch, seq, heads,
                    2 * d_lru // HEAD_DIM, min(seq, 2048), 4)
    wo = w_out[l].astype(BF16)
    x1, h2, h2p = _out_proj(y_lru, y_hgrn, wo[:d_lru], wo[d_lru:], x, gt1, norm2_g[l][None, :], sc2, sh2, 512)

    idx, w_sel, rank, counts = _route(h2, w_router[l].T.astype(BF16), router_bias[l][:, None], 512)
    counts = counts[:, 0]
    bm = EXPERT_ROWS
    pcounts = (counts + bm - 1) // bm * bm
    pends = jnp.cumsum(pcounts)
    pstarts = pends - pcounts
    n_rows = -(-(t * TOP_K + ne * (bm - 1)) // bm) * bm
    dest = _slots(pstarts, idx, rank, min(2048, t)).T.reshape(-1)
    xs = _dispatch(counts, pstarts, pcounts, dest, h2p, n_rows, 256)
    ys = _experts(*_block_tables(pends, pstarts + counts, n_rows // bm), xs, w_gate[l], w_up[l], w_down[l])
    out = _combine(dest, w_sel.T, x1, h2, gt2, final_g[None, :], ws_gate[l].astype(BF16),
                   ws_up[l].astype(BF16), ws_down[l].astype(BF16), ys, seq, 256)
    return out.reshape(batch, seq, d)
```
